```python
import jax, jax.numpy as jnp
from jax import lax
import numpy as np

D_MODEL = 1024
BATCH = 8
SEQ = 2048
DEPTH = 4
DEC_BATCH = 128
DEC_SEQ = 1
PAST_LEN = 16384
PAGE_SIZE = 128

N_MIXERS = 3
EPS = 1e-6
POOL_WINDOWS = (2, 4, 8, 16)
POOL_GROUPS = len(POOL_WINDOWS)
D_POOL = 2 * D_MODEL
POOL_GW = D_POOL // POOL_GROUPS
POOL_HIST = max(POOL_WINDOWS) - 1
SGU_CHUNK = 128
D_SG = 2 * D_MODEL
SGU_GROUPS = 4
SGU_GW = D_SG // SGU_GROUPS
DN_DK = 128
DN_DV = 128
DN_HEADS = D_MODEL // DN_DK
DN_QK = DN_HEADS * DN_DK
DN_V = DN_HEADS * DN_DV
DN_CONV_CH = 2 * DN_QK + DN_V
DN_CONV = 4
DN_CHUNK = 64
DN_PROJ = DN_CONV_CH + DN_V + 2 * DN_HEADS

kernel_name = "hybrid_pool_sgu_gdn_decoder_step"

F32 = jnp.float32


def rmsnorm(x, gain):
    xf = x.astype(F32)
    y = xf * lax.rsqrt(jnp.mean(xf * xf, axis=-1, keepdims=True) + EPS)
    return (y * gain.astype(F32)).astype(x.dtype)


def l2norm(x):
    xf = x.astype(F32)
    return xf * lax.rsqrt(jnp.sum(xf * xf, axis=-1, keepdims=True) + EPS)


def pool_mixer(h, hist, start_pos, w_in, w_grp, scale, w_out):
    b, t, _ = h.shape
    xb, z = jnp.split(h @ w_in, 2, axis=-1)
    x_ext = jnp.concatenate([hist.astype(xb.dtype), xb], axis=1)
    cs = jnp.cumsum(jnp.pad(x_ext.astype(F32), ((0, 0), (1, 0), (0, 0))), axis=1)
    pos = start_pos + jnp.arange(t)
    means = []
    for gi, w in enumerate(POOL_WINDOWS):
        c0, c1 = gi * POOL_GW, (gi + 1) * POOL_GW
        win = cs[:, POOL_HIST + 1:POOL_HIST + 1 + t, c0:c1] - cs[:, POOL_HIST + 1 - w:POOL_HIST + 1 - w + t, c0:c1]
        cnt = jnp.minimum(pos + 1, w).astype(F32)
        means.append(win / cnt[None, :, None])
    pooled = jnp.concatenate(means, axis=-1) - xb.astype(F32)
    mixed = jnp.einsum('btgc,gcd->btgd', pooled.reshape(b, t, POOL_GROUPS, POOL_GW), w_grp.astype(F32))
    mixed = mixed.reshape(b, t, D_POOL) * scale.astype(F32) * jax.nn.silu(z.astype(F32))
    y = mixed.astype(h.dtype) @ w_out
    return y, x_ext[:, -POOL_HIST:]


def sgu_mixer(h, w_in, ln_g, ln_b, w_s, b_s, w_out):
    b, t, _ = h.shape
    u, v, z = jnp.split(h @ w_in, 3, axis=-1)
    u = jax.nn.gelu(u.astype(F32), approximate=False)
    v = jax.nn.gelu(v.astype(F32), approximate=False)
    mu = jnp.mean(v, axis=-1, keepdims=True)
    var = jnp.mean(jnp.square(v - mu), axis=-1, keepdims=True)
    v = (v - mu) * lax.rsqrt(var + EPS) * ln_g.astype(F32) + ln_b.astype(F32)
    n = -(-t // SGU_CHUNK)
    vp = jnp.pad(v, ((0, 0), (0, n * SGU_CHUNK - t), (0, 0))).reshape(b, n, SGU_CHUNK, SGU_GROUPS, SGU_GW)
    causal = jnp.tril(jnp.ones((SGU_CHUNK, SGU_CHUNK), dtype=bool))
    ws = jnp.where(causal, w_s.astype(F32), 0.0)
    s = jnp.einsum('gij,bnjgc->bnigc', ws, vp) + b_s.astype(F32).T[None, None, :, :, None]
    s = s.reshape(b, n * SGU_CHUNK, D_SG)[:, :t]
    y = (u * s * jax.nn.silu(z.astype(F32))).astype(h.dtype) @ w_out
    return y, v.astype(h.dtype)


def gated_delta_chunked(q, k, v, g, beta, s0):
    b, t = q.shape[:2]
    n = -(-t // DN_CHUNK)
    pad = n * DN_CHUNK - t

    def to_chunks(a):
        a = jnp.pad(a, ((0, 0), (0, pad)) + ((0, 0),) * (a.ndim - 2))
        a = a.reshape((b, n, DN_CHUNK) + a.shape[2:])
        return jnp.moveaxis(jnp.moveaxis(a, 3, 2), 1, 0)

    qc, kc, vc, gc, bc = [to_chunks(a) for a in (q, k, v, g, beta)]
    gam = jnp.cumsum(gc, axis=-1)
    idx = jnp.arange(DN_CHUNK)
    incl = idx[:, None] >= idx[None, :]
    strict = idx[:, None] > idx[None, :]
    dec = jnp.exp(jnp.where(incl, gam[..., :, None] - gam[..., None, :], -jnp.inf))
    m = jnp.where(strict, bc[..., :, None] * jnp.einsum('nbhid,nbhjd->nbhij', kc, kc) * dec, 0.0)
    rhs = jnp.concatenate([bc[..., None] * vc, (bc * jnp.exp(gam))[..., None] * kc], axis=-1)
    sol = lax.linalg.triangular_solve(m + jnp.eye(DN_CHUNK, dtype=m.dtype), rhs,
                                      left_side=True, lower=True, unit_diagonal=True)
    u_val, w_k = sol[..., :DN_DV], sol[..., DN_DV:]
    qk = jnp.einsum('nbhid,nbhjd->nbhij', qc, kc) * dec
    q_dec = qc * jnp.exp(gam)[..., None]
    k_tail = kc * jnp.exp(gam[..., -1:] - gam)[..., None]
    a_last = jnp.exp(gam[..., -1])

    def step(s, xs):
        u_c, wk_c, qk_c, qd_c, kt_c, al_c = xs
        w = u_c - jnp.einsum('bhck,bhkv->bhcv', wk_c, s)
        o = jnp.einsum('bhck,bhkv->bhcv', qd_c, s) + jnp.einsum('bhij,bhjv->bhiv', qk_c, w)
        s = al_c[..., None, None] * s + jnp.einsum('bhck,bhcv->bhkv', kt_c, w)
        return s, o

    s_fin, o = lax.scan(step, s0.astype(F32), (u_val, w_k, qk, q_dec, k_tail, a_last))
    o = jnp.moveaxis(jnp.moveaxis(o, 0, 1), 2, 3)
    o = o.reshape(b, n * DN_CHUNK, DN_HEADS, DN_DV)[:, :t]
    return o, s_fin


def delta_mixer(h, conv_hist, s0, w_in, conv_w, a_log, dt_bias, o_gain, w_out):
    b, t, _ = h.shape
    proj = h @ w_in
    qkv = proj[..., :DN_CONV_CH]
    z = proj[..., DN_CONV_CH:DN_CONV_CH + DN_V]
    a = proj[..., DN_CONV_CH + DN_V:DN_CONV_CH + DN_V + DN_HEADS]
    bb = proj[..., DN_CONV_CH + DN_V + DN_HEADS:]
    x_ext = jnp.concatenate([conv_hist.astype(qkv.dtype), qkv], axis=1)
    cw = conv_w.astype(F32)
    conv = sum(x_ext[:, j:j + t].astype(F32) * cw[j] for j in range(DN_CONV))
    conv = jax.nn.silu(conv)
    q = l2norm(conv[..., :DN_QK].reshape(b, t, DN_HEADS, DN_DK)) * (DN_DK ** -0.5)
    k = l2norm(conv[..., DN_QK:2 * DN_QK].reshape(b, t, DN_HEADS, DN_DK))
    v = conv[..., 2 * DN_QK:].reshape(b, t, DN_HEADS, DN_DV)
    g = -jnp.exp(a_log.astype(F32)) * jax.nn.softplus(a.astype(F32) + dt_bias.astype(F32))
    beta = jax.nn.sigmoid(bb.astype(F32))
    o, s_new = gated_delta_chunked(q, k, v, g, beta, s0)
    o = rmsnorm(o, o_gain).reshape(b, t, DN_V) * jax.nn.silu(z.astype(F32))
    y = o.astype(h.dtype) @ w_out
    return y, x_ext[:, -(DN_CONV - 1):], s_new


def setup_inputs(seed: int = 0) -> dict:
    key = jax.random.key(seed)
    ks = iter(jax.random.split(key, 64))

    def nrm(shape, scale):
        return jax.random.normal(next(ks), shape, F32) * scale

    def gain(n):
        return 1.0 + nrm((n,), 0.02)

    def pool_params():
        return (nrm((D_MODEL, 2 * D_POOL), D_MODEL ** -0.5),
                nrm((POOL_GROUPS, POOL_GW, POOL_GW), POOL_GW ** -0.5),
                gain(D_POOL),
                nrm((D_POOL, D_MODEL), D_POOL ** -0.5))

    inp = {}
    inp["x_prompt"] = nrm((BATCH, SEQ, D_MODEL), 1.0)
    inp["x_sample"] = nrm((DEC_BATCH, DEC_SEQ, D_MODEL), 1.0)
    inp["state_pool_l0"] = nrm((DEC_BATCH, POOL_HIST, D_POOL), 1.0)
    inp["state_conv_l2"] = nrm((DEC_BATCH, DN_CONV - 1, DN_CONV_CH), 1.0)
    inp["state_delta_l2"] = nrm((DEC_BATCH, DN_HEADS, DN_DK, DN_DV), 0.3)
    inp["state_pool_l3"] = nrm((DEC_BATCH, POOL_HIST, D_POOL), 1.0)

    inp["l0_norm"] = gain(D_MODEL)
    p = pool_params()
    inp["l0_pool_w_in"], inp["l0_pool_w_grp"], inp["l0_pool_scale"], inp["l0_pool_w_out"] = p

    inp["l1_norm"] = gain(D_MODEL)
    inp["l1_sgu_w_in"] = nrm((D_MODEL, 3 * D_SG), D_MODEL ** -0.5)
    inp["l1_sgu_ln_g"] = gain(D_SG)
    inp["l1_sgu_ln_b"] = nrm((D_SG,), 0.02)
    inp["l1_sgu_w_s"] = nrm((SGU_GROUPS, SGU_CHUNK, SGU_CHUNK), SGU_CHUNK ** -0.5)
    inp["l1_sgu_b_s"] = nrm((SGU_GROUPS, SGU_CHUNK), 0.1)
    inp["l1_sgu_w_out"] = nrm((D_SG, D_MODEL), D_SG ** -0.5)

    inp["l2_norm"] = gain(D_MODEL)
    inp["l2_dn_w_in"] = nrm((D_MODEL, DN_PROJ), D_MODEL ** -0.5)
    inp["l2_dn_conv_w"] = nrm((DN_CONV, DN_CONV_CH), DN_CONV ** -0.5)
    inp["l2_dn_a_log"] = jnp.log(jax.random.uniform(next(ks), (DN_HEADS,), F32, 1.0, 16.0))
    dt = jnp.exp(jax.random.uniform(next(ks), (DN_HEADS,), F32) * (jnp.log(0.1) - jnp.log(0.001)) + jnp.log(0.001))
    inp["l2_dn_dt_bias"] = dt + jnp.log(-jnp.expm1(-dt))
    inp["l2_dn_o_gain"] = gain(DN_DV)
    inp["l2_dn_w_out"] = nrm((DN_V, D_MODEL), DN_V ** -0.5)

    inp["l3_norm"] = gain(D_MODEL)
    p = pool_params()
    inp["l3_pool_w_in"], inp["l3_pool_w_grp"], inp["l3_pool_scale"], inp["l3_pool_w_out"] = p

    inp["final_norm"] = gain(D_MODEL)
    return inp


def reference(x_prompt, x_sample, state_pool_l0, state_conv_l2, state_delta_l2, state_pool_l3,
              l0_norm, l0_pool_w_in, l0_pool_w_grp, l0_pool_scale, l0_pool_w_out,
              l1_norm, l1_sgu_w_in, l1_sgu_ln_g, l1_sgu_ln_b, l1_sgu_w_s, l1_sgu_b_s, l1_sgu_w_out,
              l2_norm, l2_dn_w_in, l2_dn_conv_w, l2_dn_a_log, l2_dn_dt_bias, l2_dn_o_gain, l2_dn_w_out,
              l3_norm, l3_pool_w_in, l3_pool_w_grp, l3_pool_scale, l3_pool_w_out,
              final_norm):
    norms = (l0_norm, l1_norm, l2_norm, l3_norm)
    layer_params = (
        (l0_pool_w_in, l0_pool_w_grp, l0_pool_scale, l0_pool_w_out),
        (l1_sgu_w_in, l1_sgu_ln_g, l1_sgu_ln_b, l1_sgu_w_s, l1_sgu_b_s, l1_sgu_w_out),
        (l2_dn_w_in, l2_dn_conv_w, l2_dn_a_log, l2_dn_dt_bias, l2_dn_o_gain, l2_dn_w_out),
        (l3_pool_w_in, l3_pool_w_grp, l3_pool_scale, l3_pool_w_out),
    )
    sample_states = ((state_pool_l0,), (), (state_conv_l2, state_delta_l2), (state_pool_l3,))
    bp = x_prompt.shape[0]
    xp, xs = x_prompt, x_sample
    new_states = []
    for i in range(DEPTH):
        hp = rmsnorm(xp, norms[i])
        hs = rmsnorm(xs, norms[i])
        prm = layer_params[i]
        kind = i % N_MIXERS
        if kind == 0:
            hist0 = jnp.zeros((bp, POOL_HIST, D_POOL), xp.dtype)
            yp, pool_p = pool_mixer(hp, hist0, 0, *prm)
            ys, pool_s = pool_mixer(hs, sample_states[i][0], PAST_LEN, *prm)
            new_states.append((pool_p, pool_s))
        elif kind == 1:
            yp, _ = sgu_mixer(hp, *prm)
            ys, v_s = sgu_mixer(hs, *prm)
            new_states.append((v_s,))
        else:
            conv0 = jnp.zeros((bp, DN_CONV - 1, DN_CONV_CH), xp.dtype)
            s0 = jnp.zeros((bp, DN_HEADS, DN_DK, DN_DV), F32)
            yp, conv_p, dn_p = delta_mixer(hp, conv0, s0, *prm)
            ys, conv_s, dn_s = delta_mixer(hs, sample_states[i][0], sample_states[i][1], *prm)
            new_states.append((conv_p, conv_s, dn_p, dn_s))
        xp = xp + yp
        xs = xs + ys
    y_prompt = rmsnorm(xp, final_norm)
    y_sample = rmsnorm(xs, final_norm)
    (pool0_p, pool0_s), (sgu1_s,), (conv2_p, conv2_s, dn2_p, dn2_s), (pool3_p, pool3_s) = new_states
    return (y_prompt, y_sample, pool0_p, pool0_s, sgu1_s, conv2_p, conv2_s, dn2_p, dn2_s, pool3_p, pool3_s)
```

```python
import functools

import jax
import jax.numpy as jnp
from jax import lax
from jax.experimental import pallas as pl
from jax.experimental.pallas import tpu as pltpu

F32 = jnp.float32
BF16 = jnp.bfloat16
EPS = 1e-6
PAST_LEN = 16384
POOL_WINDOWS = (2, 4, 8, 16)
POOL_HIST = 15
POOL_HIST_PAD = 16
SGU_CHUNK = 128
SGU_GROUPS = 4
DN_HEADS = 8
DN_DK = 128
DN_DV = 128
DN_CONV = 4
CONV_HIST_PAD = 8
DN_CHUNK = 128
INV_BASE_BITS = 4
LANES = 128
VMEM_LIMIT = 60 * 1024 * 1024


def _rmsnorm(x, gain):
    ms = jnp.mean(x * x, axis=-1, keepdims=True)
    return x * lax.rsqrt(ms + EPS) * gain


def _silu(z):
    return z * jax.nn.sigmoid(z)


def _gelu(x):
    return 0.5 * x * (1.0 + lax.erf(x * (2.0 ** -0.5)))


def _dot(a, b):
    return jnp.dot(a, b, preferred_element_type=F32)


def _dot_nt(a, b):
    return lax.dot_general(a, b, (((1,), (1,)), ((), ())), preferred_element_type=F32)


def _dot_tn(a, b):
    return lax.dot_general(a, b, (((0,), (0,)), ((), ())), preferred_element_type=F32)


def _const_spec(shape):
    nd = len(shape)
    return pl.BlockSpec(shape, lambda *_: (0,) * nd, pipeline_mode=pl.Buffered(1))


def _params(n_grid):
    return pltpu.CompilerParams(
        dimension_semantics=("arbitrary",) * n_grid, vmem_limit_bytes=VMEM_LIMIT)


def _pool_tail(pooled_groups, z, x, wgrp_ref, scale_ref, wout_ref, fin_ref, final_norm):
    mixed = [_dot(p.astype(BF16), wgrp_ref[gi]) for gi, p in enumerate(pooled_groups)]
    mixed = jnp.concatenate(mixed, axis=1)
    gated = mixed * scale_ref[...] * _silu(z)
    y = _dot(gated.astype(BF16), wout_ref[...]) + x
    if final_norm:
        y = _rmsnorm(y, fin_ref[...])
    return y


def _pool_prompt_body(x_ref, gain_ref, win_ref, wgrp_ref, scale_ref, wout_ref, fin_ref,
                      y_ref, st_ref, hist_ref, *, tm, n_t, final_norm):
    t = pl.program_id(1)
    dp = scale_ref.shape[-1]
    gw = dp // len(POOL_WINDOWS)

    @pl.when(t == 0)
    def _():
        hist_ref[...] = jnp.zeros_like(hist_ref)

    x = x_ref[...]
    h = _rmsnorm(x, gain_ref[...]).astype(BF16)
    xz = _dot(h, win_ref[...])
    xb = xz[:, :dp]
    z = xz[:, dp:]
    ext = jnp.concatenate([hist_ref[...], xb], axis=0)
    last = xb[tm - POOL_HIST_PAD:, :]
    hist_ref[...] = last
    pos1 = lax.broadcasted_iota(jnp.int32, (tm, LANES), 0) + (t * tm + 1)
    pooled = []
    for gi, w in enumerate(POOL_WINDOWS):
        e = ext[:, gi * gw:(gi + 1) * gw]
        s = e + pltpu.roll(e, 1, 0)
        sh = 2
        while sh < w:
            s = s + pltpu.roll(s, sh, 0)
            sh *= 2
        s = s[POOL_HIST_PAD:, :]
        inv = 1.0 / jnp.minimum(pos1, w).astype(F32)
        inv = jnp.concatenate([inv] * (gw // LANES), axis=1)
        pooled.append(s * inv - xb[:, gi * gw:(gi + 1) * gw])
    y_ref[...] = _pool_tail(pooled, z, x, wgrp_ref, scale_ref, wout_ref, fin_ref, final_norm)

    @pl.when(t == n_t - 1)
    def _():
        st_ref[...] = last


def _pool_prompt(x, gain, win, wgrp, scale, wout, fin, *, tm, final_norm):
    b, t, d = x.shape
    dp = scale.shape[-1]
    n_t = t // tm
    body = functools.partial(_pool_prompt_body, tm=tm, n_t=n_t, final_norm=final_norm)
    return pl.pallas_call(
        body,
        grid=(b, n_t),
        in_specs=[
            pl.BlockSpec((None, tm, d), lambda i, j: (i, j, 0)),
            _const_spec(gain.shape), _const_spec(win.shape), _const_spec(wgrp.shape),
            _const_spec(scale.shape), _const_spec(wout.shape), _const_spec(fin.shape),
        ],
        out_specs=[
            pl.BlockSpec((None, tm, d), lambda i, j: (i, j, 0)),
            pl.BlockSpec((None, POOL_HIST_PAD, dp), lambda i, j: (i, 0, 0)),
        ],
        out_shape=[jax.ShapeDtypeStruct((b, t, d), F32),
                   jax.ShapeDtypeStruct((b, POOL_HIST_PAD, dp), F32)],
        scratch_shapes=[pltpu.VMEM((POOL_HIST_PAD, dp), F32)],
        compiler_params=_params(2),
        name="pool_prompt",
    )(x, gain, win, wgrp, scale, wout, fin)


def _pool_sample_body(x_ref, st_ref, gain_ref, win_ref, wgrp_ref, scale_ref, wout_ref, fin_ref,
                      y_ref, nst_ref, *, final_norm):
    dp = scale_ref.shape[-1]
    gw = dp // len(POOL_WINDOWS)
    x = x_ref[...]
    h = _rmsnorm(x, gain_ref[...]).astype(BF16)
    xz = _dot(h, win_ref[...])
    xb = xz[:, :dp]
    z = xz[:, dp:]
    pooled = []
    for gi, w in enumerate(POOL_WINDOWS):
        xg = xb[:, gi * gw:(gi + 1) * gw]
        s = xg
        for k in range(1, w):
            c0 = (POOL_HIST - k) * dp + gi * gw
            s = s + st_ref[:, c0:c0 + gw]
        cnt = float(min(PAST_LEN + 1, w))
        pooled.append(s * (1.0 / cnt) - xg)
    y_ref[...] = _pool_tail(pooled, z, x, wgrp_ref, scale_ref, wout_ref, fin_ref, final_norm)
    nst_ref[:, :(POOL_HIST - 1) * dp] = st_ref[:, dp:]
    nst_ref[:, (POOL_HIST - 1) * dp:] = xb


def _pool_sample(x, st, gain, win, wgrp, scale, wout, fin, *, final_norm):
    nb, d = x.shape
    body = functools.partial(_pool_sample_body, final_norm=final_norm)
    return pl.pallas_call(
        body,
        out_shape=[jax.ShapeDtypeStruct((nb, d), F32), jax.ShapeDtypeStruct(st.shape, F32)],
        compiler_params=_params(0),
        name="pool_sample",
    )(x, st, gain, win, wgrp, scale, wout, fin)


def _sgu_front(x, gain_ref, win_ref, lng_ref, lnb_ref):
    ds = lng_ref.shape[-1]
    h = _rmsnorm(x, gain_ref[...]).astype(BF16)
    uvz = _dot(h, win_ref[...])
    u = _gelu(uvz[:, :ds])
    v = _gelu(uvz[:, ds:2 * ds])
    z = uvz[:, 2 * ds:]
    mu = jnp.mean(v, axis=-1, keepdims=True)
    vc = v - mu
    var = jnp.mean(vc * vc, axis=-1, keepdims=True)
    vn = vc * lax.rsqrt(var + EPS) * lng_ref[...] + lnb_ref[...]
    return u, vn, z


def _sgu_prompt_body(x_ref, gain_ref, win_ref, lng_ref, lnb_ref, ws_ref, bsb_ref, wout_ref,
                     y_ref, *, tm):
    ds = lng_ref.shape[-1]
    gw = ds // SGU_GROUPS
    x = x_ref[...]
    u, vn, z = _sgu_front(x, gain_ref, win_ref, lng_ref, lnb_ref)
    ri = lax.broadcasted_iota(jnp.int32, (SGU_CHUNK, SGU_CHUNK), 0)
    ci = lax.broadcasted_iota(jnp.int32, (SGU_CHUNK, SGU_CHUNK), 1)
    ws = [jnp.where(ri >= ci, ws_ref[g], 0.0).astype(BF16) for g in range(SGU_GROUPS)]
    vnb = vn.astype(BF16)
    bsb = bsb_ref[...]
    rows = []
    for n in range(tm // SGU_CHUNK):
        r0 = n * SGU_CHUNK
        cols = [_dot(ws[g], vnb[r0:r0 + SGU_CHUNK, g * gw:(g + 1) * gw]) for g in range(SGU_GROUPS)]
        rows.append(jnp.concatenate(cols, axis=1) + bsb)
    s = jnp.concatenate(rows, axis=0)
    gated = u * s * _silu(z)
    y_ref[...] = _dot(gated.astype(BF16), wout_ref[...]) + x


def _sgu_prompt(x, gain, win, lng, lnb, ws, bsb, wout, *, tm):
    n, d = x.shape
    body = functools.partial(_sgu_prompt_body, tm=tm)
    return pl.pallas_call(
        body,
        grid=(n // tm,),
        in_specs=[pl.BlockSpec((tm, d), lambda i: (i, 0))] + [
            _const_spec(a.shape) for a in (gain, win, lng, lnb, ws, bsb, wout)],
        out_specs=pl.BlockSpec((tm, d), lambda i: (i, 0)),
        out_shape=jax.ShapeDtypeStruct((n, d), F32),
        compiler_params=_params(1),
        name="sgu_prompt",
    )(x, gain, win, lng, lnb, ws, bsb, wout)


def _sgu_sample_body(x_ref, gain_ref, win_ref, lng_ref, lnb_ref, w00_ref, b0_ref, wout_ref,
                     y_ref, v_ref):
    x = x_ref[...]
    u, vn, z = _sgu_front(x, gain_ref, win_ref, lng_ref, lnb_ref)
    s = vn * w00_ref[...] + b0_ref[...]
    gated = u * s * _silu(z)
    y_ref[...] = _dot(gated.astype(BF16), wout_ref[...]) + x
    v_ref[...] = vn


def _sgu_sample(x, gain, win, lng, lnb, w00, b0, wout):
    nb, d = x.shape
    ds = lng.shape[-1]
    return pl.pallas_call(
        _sgu_sample_body,
        out_shape=[jax.ShapeDtypeStruct((nb, d), F32), jax.ShapeDtypeStruct((nb, ds), F32)],
        compiler_params=_params(0),
        name="sgu_sample",
    )(x, gain, win, lng, lnb, w00, b0, wout)


def _gdn_gates(ab, alog_ref, dtb_ref):
    g = -jnp.exp(alog_ref[...]) * jax.nn.softplus(ab + dtb_ref[...])
    beta = jax.nn.sigmoid(ab)
    return g, beta


def _lane_pick(a, lane, idx):
    return jnp.sum(jnp.where(lane == idx, a, 0.0), axis=-1, keepdims=True)


def _gdn_heads(c, g_like, beta_all, emit):
    rows = c.shape[0]
    qk = DN_HEADS * DN_DK
    lane = lax.broadcasted_iota(jnp.int32, (rows, LANES), 1)
    for hh in range(DN_HEADS):
        qh = c[:, hh * DN_DK:(hh + 1) * DN_DK]
        kh = c[:, qk + hh * DN_DK:qk + (hh + 1) * DN_DK]
        vh = c[:, 2 * qk + hh * DN_DV:2 * qk + (hh + 1) * DN_DV]
        qn = qh * lax.rsqrt(jnp.sum(qh * qh, axis=-1, keepdims=True) + EPS) * (DN_DK ** -0.5)
        kn = kh * lax.rsqrt(jnp.sum(kh * kh, axis=-1, keepdims=True) + EPS)
        emit(hh, qn, kn, vh, _lane_pick(g_like, lane, hh), _lane_pick(beta_all, lane, DN_HEADS + hh))


def _gdn_proj_prompt_body(x_ref, gain_ref, win_ref, cw_ref, alog_ref, dtb_ref,
                          q_ref, k_ref, kb_ref, vb_ref, gam_ref, z_ref, cst_ref, hist_ref,
                          *, tm, n_t):
    t = pl.program_id(1)
    qk = DN_HEADS * DN_DK
    cch = 3 * qk

    @pl.when(t == 0)
    def _():
        hist_ref[...] = jnp.zeros_like(hist_ref)

    x = x_ref[...]
    h = _rmsnorm(x, gain_ref[...]).astype(BF16)
    proj = _dot(h, win_ref[...])
    qkv = proj[:, :cch]
    z_ref[...] = proj[:, cch:cch + qk]
    ab = proj[:, cch + qk:cch + qk + LANES]
    ext = jnp.concatenate([hist_ref[...], qkv], axis=0)
    last = qkv[tm - CONV_HIST_PAD:, :]
    hist_ref[...] = last
    cw = cw_ref[...]
    conv = ext * cw[DN_CONV - 1:DN_CONV, :]
    for j in range(1, DN_CONV):
        conv = conv + pltpu.roll(ext, j, 0) * cw[DN_CONV - 1 - j:DN_CONV - j, :]
    c = _silu(conv[CONV_HIST_PAD:, :])

    g, beta = _gdn_gates(ab, alog_ref, dtb_ref)
    rin = lax.broadcasted_iota(jnp.int32, (tm, LANES), 0) & (DN_CHUNK - 1)
    gam = g
    sh = 1
    while sh < DN_CHUNK:
        gam = gam + jnp.where(rin >= sh, pltpu.roll(gam, sh, 0), 0.0)
        sh *= 2

    def emit(hh, qn, kn, vh, g_col, b_col):
        sl = slice(hh * DN_DK, (hh + 1) * DN_DK)
        q_ref[:, sl] = qn.astype(BF16)
        k_ref[:, sl] = kn.astype(BF16)
        kb_ref[:, sl] = (b_col * kn).astype(BF16)
        vb_ref[:, sl] = (b_col * vh).astype(BF16)
        gam_ref[:, sl] = jnp.broadcast_to(g_col, (tm, DN_DK))

    _gdn_heads(c, gam, beta, emit)

    @pl.when(t == n_t - 1)
    def _():
        cst_ref[...] = last


def _gdn_proj_prompt(x, gain, win, cw, alog, dtb, *, tm):
    b, t, d = x.shape
    qk = DN_HEADS * DN_DK
    cch = 3 * qk
    n_t = t // tm
    body = functools.partial(_gdn_proj_prompt_body, tm=tm, n_t=n_t)
    tok = lambda i, j: (i, j, 0)
    return pl.pallas_call(
        body,
        grid=(b, n_t),
        in_specs=[pl.BlockSpec((None, tm, d), tok)] + [
            _const_spec(a.shape) for a in (gain, win, cw, alog, dtb)],
        out_specs=[pl.BlockSpec((None, tm, qk), tok)] * 6 + [
            pl.BlockSpec((None, CONV_HIST_PAD, cch), lambda i, j: (i, 0, 0))],
        out_shape=[jax.ShapeDtypeStruct((b, t, qk), BF16)] * 4 + [
            jax.ShapeDtypeStruct((b, t, qk), F32)] * 2 + [
            jax.ShapeDtypeStruct((b, CONV_HIST_PAD, cch), F32)],
        scratch_shapes=[pltpu.VMEM((CONV_HIST_PAD, cch), F32)],
        compiler_params=_params(2),
        name="gdn_proj_prompt",
    )(x, gain, win, cw, alog, dtb)


def _unit_lower_inverse_minus_eye(m, in_base, at_level):
    a = jnp.where(in_base, m, 0.0)
    e = -a
    ab = a.astype(BF16)
    p = _dot(ab, ab)
    for it in range(INV_BASE_BITS - 1):
        pb = p.astype(BF16)
        e = e + p + _dot(e.astype(BF16), pb)
        if it < INV_BASE_BITS - 2:
            p = _dot(pb, pb)
    for mask in at_level:
        low = jnp.where(mask, m, 0.0)
        eb = e.astype(BF16)
        x = low + _dot(low.astype(BF16), eb)
        e = e - (x + _dot(eb, x.astype(BF16)))
    return e


def _gdn_out(o_heads, z, x, og_ref, wout_ref):
    normed = [oh * lax.rsqrt(jnp.mean(oh * oh, axis=-1, keepdims=True) + EPS) * og_ref[...]
              for oh in o_heads]
    gated = jnp.concatenate(normed, axis=1) * _silu(z)
    return _dot(gated.astype(BF16), wout_ref[...]) + x


def _gdn_delta_prompt_body(q_ref, k_ref, kb_ref, vb_ref, gam_ref, z_ref, x_ref, og_ref, wout_ref,
                           y_ref, sfin_ref, s_ref, o_ref, *, tm, n_t):
    t = pl.program_id(1)

    @pl.when(t == 0)
    def _():
        s_ref[...] = jnp.zeros_like(s_ref)

    c = DN_CHUNK
    ri = lax.broadcasted_iota(jnp.int32, (c, c), 0)
    ci = lax.broadcasted_iota(jnp.int32, (c, c), 1)
    incl = ri >= ci
    strict = ri > ci
    top = ri ^ ci
    in_base = (top >> INV_BASE_BITS) == 0
    levels = range(INV_BASE_BITS, c.bit_length() - 1)
    at_level = [(top >> lv) == 1 for lv in levels]
    for n in range(tm // c):
        rs = slice(n * c, (n + 1) * c)
        for hh in range(DN_HEADS):
            hs = slice(hh * DN_DK, (hh + 1) * DN_DK)
            qh = q_ref[rs, hs]
            kh = k_ref[rs, hs]
            kbh = kb_ref[rs, hs]
            vbh = vb_ref[rs, hs]
            gam = gam_ref[rs, hs]
            dec = jnp.exp(jnp.where(incl, gam - gam.T, -1e30))
            m = jnp.where(strict, _dot_nt(kbh, kh) * dec, 0.0)
            qkm = _dot_nt(qh, kh) * dec
            e = _unit_lower_inverse_minus_eye(m, in_base, at_level)
            eg = jnp.exp(gam)
            rhs = jnp.concatenate([vbh.astype(F32), kbh.astype(F32) * eg], axis=1)
            uw = rhs + _dot(e.astype(BF16), rhs.astype(BF16))
            u_val = uw[:, :DN_DV]
            w_k = uw[:, DN_DV:]
            gl = gam[c - 1:c, :]
            qd = (qh.astype(F32) * eg).astype(BF16)
            kt = (kh.astype(F32) * jnp.exp(gl - gam)).astype(BF16)
            s_old = s_ref[hh]
            sb = s_old.astype(BF16)
            w = u_val - _dot(w_k.astype(BF16), sb)
            wb = w.astype(BF16)
            o_ref[rs, hs] = _dot(qd, sb) + _dot(qkm.astype(BF16), wb)
            s_ref[hh] = jnp.exp(gl) * s_old + _dot_tn(kt, wb)
    o = o_ref[...]
    o_heads = [o[:, hh * DN_DV:(hh + 1) * DN_DV] for hh in range(DN_HEADS)]
    y_ref[...] = _gdn_out(o_heads, z_ref[...], x_ref[...], og_ref, wout_ref)

    @pl.when(t == n_t - 1)
    def _():
        sfin_ref[...] = s_ref[...]


def _gdn_delta_prompt(q, k, kb, vb, gam, z, x, og, wout, *, tm):
    b, t, d = x.shape
    qk = DN_HEADS * DN_DK
    n_t = t // tm
    body = functools.partial(_gdn_delta_prompt_body, tm=tm, n_t=n_t)
    tok = lambda i, j: (i, j, 0)
    return pl.pallas_call(
        body,
        grid=(b, n_t),
        in_specs=[pl.BlockSpec((None, tm, qk), tok)] * 6 + [pl.BlockSpec((None, tm, d), tok),
                  _const_spec(og.shape), _const_spec(wout.shape)],
        out_specs=[pl.BlockSpec((None, tm, d), tok),
                   pl.BlockSpec((None, DN_HEADS, DN_DK, DN_DV), lambda i, j: (i, 0, 0, 0))],
        out_shape=[jax.ShapeDtypeStruct((b, t, d), F32),
                   jax.ShapeDtypeStruct((b, DN_HEADS, DN_DK, DN_DV), F32)],
        scratch_shapes=[pltpu.VMEM((DN_HEADS, DN_DK, DN_DV), F32),
                        pltpu.VMEM((tm, DN_HEADS * DN_DV), F32)],
        compiler_params=_params(2),
        name="gdn_delta_prompt",
    )(q, k, kb, vb, gam, z, x, og, wout)


def _gdn_proj_sample_body(x_ref, cst_ref, gain_ref, win_ref, cw_ref, alog_ref, dtb_ref,
                          q_ref, k_ref, v_ref, eg_ref, beta_ref, z_ref, ncst_ref):
    qk = DN_HEADS * DN_DK
    cch = 3 * qk
    nb = x_ref.shape[0]
    x = x_ref[...]
    h = _rmsnorm(x, gain_ref[...]).astype(BF16)
    proj = _dot(h, win_ref[...])
    qkv = proj[:, :cch]
    z_ref[...] = proj[:, cch:cch + qk]
    ab = proj[:, cch + qk:cch + qk + LANES]
    cw = cw_ref[...]
    conv = qkv * cw[DN_CONV - 1:DN_CONV, :]
    for j in range(DN_CONV - 1):
        conv = conv + cst_ref[:, j * cch:(j + 1) * cch] * cw[j:j + 1, :]
    c = _silu(conv)
    ncst_ref[:, :(DN_CONV - 2) * cch] = cst_ref[:, cch:]
    ncst_ref[:, (DN_CONV - 2) * cch:] = qkv
    g, beta = _gdn_gates(ab, alog_ref, dtb_ref)

    def emit(hh, qn, kn, vh, g_col, b_col):
        sl = slice(hh * DN_DK, (hh + 1) * DN_DK)
        q_ref[:, sl] = qn
        k_ref[:, sl] = kn
        v_ref[:, sl] = vh
        eg_ref[:, sl] = jnp.broadcast_to(jnp.exp(g_col), (nb, DN_DK))
        beta_ref[:, sl] = jnp.broadcast_to(b_col, (nb, DN_DK))

    _gdn_heads(c, g, beta, emit)


def _gdn_proj_sample(x, cst, gain, win, cw, alog, dtb):
    nb, d = x.shape
    qk = DN_HEADS * DN_DK
    return pl.pallas_call(
        _gdn_proj_sample_body,
        out_shape=[jax.ShapeDtypeStruct((nb, qk), F32)] * 6 + [jax.ShapeDtypeStruct(cst.shape, F32)],
        compiler_params=_params(0),
        name="gdn_proj_sample",
    )(x, cst, gain, win, cw, alog, dtb)


def _gdn_state_sample_body(q_ref, k_ref, v_ref, eg_ref, beta_ref, s_ref, o_ref, ns_ref, *, nblk):
    hk = DN_HEADS * DN_DK
    row = lax.broadcasted_iota(jnp.int32, (DN_HEADS, hk), 0)
    blk = lax.broadcasted_iota(jnp.int32, (DN_HEADS, hk), 1) // DN_DK
    own = row == blk

    def blockdiag(a):
        return jnp.where(own, jnp.concatenate([a] * DN_HEADS, axis=1), 0.0).astype(BF16)

    for r in range(nblk):
        s_old = s_ref[r]
        kbd = blockdiag(k_ref[r])
        eg = eg_ref[r]
        kts = _dot(kbd, s_old.reshape(hk, DN_DV).astype(BF16))
        w = beta_ref[r] * (v_ref[r] - eg * kts)
        outer = _dot_tn(kbd, w.astype(BF16))
        s_new = s_old * eg[:, None, :] + outer.reshape(DN_HEADS, DN_DK, DN_DV)
        ns_ref[r] = s_new
        o_ref[r] = _dot(blockdiag(q_ref[r]), s_new.reshape(hk, DN_DV).astype(BF16))


def _gdn_state_sample(q, k, v, eg, beta, s, *, nblk):
    nb = s.shape[0]
    body = functools.partial(_gdn_state_sample_body, nblk=nblk)
    vec = pl.BlockSpec((nblk, DN_HEADS, DN_DK), lambda i: (i, 0, 0))
    st = pl.BlockSpec((nblk, DN_HEADS, DN_DK, DN_DV), lambda i: (i, 0, 0, 0))
    return pl.pallas_call(
        body,
        grid=(nb // nblk,),
        in_specs=[vec] * 5 + [st],
        out_specs=[vec, st],
        out_shape=[jax.ShapeDtypeStruct((nb, DN_HEADS, DN_DV), F32),
                   jax.ShapeDtypeStruct(s.shape, F32)],
        compiler_params=_params(1),
        name="gdn_state_sample",
    )(q, k, v, eg, beta, s)


def _gdn_out_sample_body(o_ref, z_ref, x_ref, og_ref, wout_ref, y_ref):
    o = o_ref[...]
    o_heads = [o[:, hh * DN_DV:(hh + 1) * DN_DV] for hh in range(DN_HEADS)]
    y_ref[...] = _gdn_out(o_heads, z_ref[...], x_ref[...], og_ref, wout_ref)


def _gdn_out_sample(o, z, x, og, wout):
    return pl.pallas_call(
        _gdn_out_sample_body,
        out_shape=jax.ShapeDtypeStruct(x.shape, F32),
        compiler_params=_params(0),
        name="gdn_out_sample",
    )(o, z, x, og, wout)


def _row(a):
    return a.reshape(1, -1).astype(F32)


def _forward(x_prompt, x_sample, state_pool_l0, state_conv_l2, state_delta_l2, state_pool_l3,
             l0_norm, l0_pool_w_in, l0_pool_w_grp, l0_pool_scale, l0_pool_w_out,
             l1_norm, l1_sgu_w_in, l1_sgu_ln_g, l1_sgu_ln_b, l1_sgu_w_s, l1_sgu_b_s, l1_sgu_w_out,
             l2_norm, l2_dn_w_in, l2_dn_conv_w, l2_dn_a_log, l2_dn_dt_bias, l2_dn_o_gain, l2_dn_w_out,
             l3_norm, l3_pool_w_in, l3_pool_w_grp, l3_pool_scale, l3_pool_w_out,
             final_norm, *, tm_pool, tm_sgu, tm_gdn, nblk_state):
    b, t, d = x_prompt.shape
    nb = x_sample.shape[0]
    dp = l0_pool_scale.shape[0]
    ds = l1_sgu_ln_g.shape[0]
    gw_s = ds // SGU_GROUPS
    qk = DN_HEADS * DN_DK
    cch = 3 * qk
    fin = _row(final_norm)
    xs = x_sample.reshape(nb, d)

    def pool_layer(xp, xs_, st, norm, w_in, w_grp, scale, w_out, final):
        args = (_row(norm), w_in.astype(BF16), w_grp.astype(BF16), _row(scale), w_out.astype(BF16), fin)
        yp, st_p = _pool_prompt(xp, *args, tm=tm_pool, final_norm=final)
        ys, st_s = _pool_sample(xs_, st.reshape(nb, POOL_HIST * dp), *args, final_norm=final)
        return yp, ys, st_p[:, POOL_HIST_PAD - POOL_HIST:, :], st_s.reshape(nb, POOL_HIST, dp)

    xp, xs, pool0_p, pool0_s = pool_layer(
        x_prompt, xs, state_pool_l0, l0_norm, l0_pool_w_in, l0_pool_w_grp, l0_pool_scale,
        l0_pool_w_out, False)

    sgu_w_in = l1_sgu_w_in.astype(BF16)
    sgu_w_out = l1_sgu_w_out.astype(BF16)
    bsb = jnp.repeat(l1_sgu_b_s.T.astype(F32), gw_s, axis=1)
    xp = _sgu_prompt(xp.reshape(b * t, d), _row(l1_norm), sgu_w_in, _row(l1_sgu_ln_g),
                     _row(l1_sgu_ln_b), l1_sgu_w_s.astype(F32), bsb, sgu_w_out,
                     tm=tm_sgu).reshape(b, t, d)
    w00 = jnp.repeat(l1_sgu_w_s[:, 0, 0].astype(F32), gw_s)[None, :]
    b0 = jnp.repeat(l1_sgu_b_s[:, 0].astype(F32), gw_s)[None, :]
    xs, sgu1_s = _sgu_sample(xs, _row(l1_norm), sgu_w_in, _row(l1_sgu_ln_g), _row(l1_sgu_ln_b),
                             w00, b0, sgu_w_out)

    n_proj = l2_dn_w_in.shape[1]
    n_pad = cch + qk + LANES
    dn_w_in = jnp.pad(l2_dn_w_in, ((0, 0), (0, n_pad - n_proj))).astype(BF16)
    dn_w_out = l2_dn_w_out.astype(BF16)
    alog = jnp.pad(l2_dn_a_log.astype(F32), (0, LANES - DN_HEADS))[None, :]
    dtb = jnp.pad(l2_dn_dt_bias.astype(F32), (0, LANES - DN_HEADS))[None, :]
    cw = l2_dn_conv_w.astype(F32)
    og = _row(l2_dn_o_gain)
    q, k, kb, vb, gam, z, conv2_p = _gdn_proj_prompt(xp, _row(l2_norm), dn_w_in, cw, alog, dtb,
                                                     tm=tm_gdn)
    xp, dn2_p = _gdn_delta_prompt(q, k, kb, vb, gam, z, xp, og, dn_w_out, tm=tm_gdn)
    conv2_p = conv2_p[:, CONV_HIST_PAD - (DN_CONV - 1):, :]

    qs, ks, vs, egs, betas, zs, conv2_s = _gdn_proj_sample(
        xs, state_conv_l2.reshape(nb, (DN_CONV - 1) * cch), _row(l2_norm), dn_w_in, cw, alog, dtb)
    hv = lambda a: a.reshape(nb, DN_HEADS, DN_DK)
    o_s, dn2_s = _gdn_state_sample(hv(qs), hv(ks), hv(vs), hv(egs), hv(betas), state_delta_l2,
                                   nblk=nblk_state)
    xs = _gdn_out_sample(o_s.reshape(nb, qk), zs, xs, og, dn_w_out)
    conv2_s = conv2_s.reshape(nb, DN_CONV - 1, cch)

    yp, ys, pool3_p, pool3_s = pool_layer(
        xp, xs, state_pool_l3, l3_norm, l3_pool_w_in, l3_pool_w_grp, l3_pool_scale,
        l3_pool_w_out, True)

    return (yp, ys.reshape(nb, 1, d), pool0_p, pool0_s, sgu1_s.reshape(nb, 1, ds),
            conv2_p, conv2_s, dn2_p, dn2_s, pool3_p, pool3_s)


def kernel(x_prompt, x_sample, state_pool_l0, state_conv_l2, state_delta_l2, state_pool_l3, l0_norm, l0_pool_w_in, l0_pool_w_grp, l0_pool_scale, l0_pool_w_out, l1_norm, l1_sgu_w_in, l1_sgu_ln_g, l1_sgu_ln_b, l1_sgu_w_s, l1_sgu_b_s, l1_sgu_w_out, l2_norm, l2_dn_w_in, l2_dn_conv_w, l2_dn_a_log, l2_dn_dt_bias, l2_dn_o_gain, l2_dn_w_out, l3_norm, l3_pool_w_in, l3_pool_w_grp, l3_pool_scale, l3_pool_w_out, final_norm):
    return _forward(
        x_prompt, x_sample, state_pool_l0, state_conv_l2, state_delta_l2, state_pool_l3,
        l0_norm, l0_pool_w_in, l0_pool_w_grp, l0_pool_scale, l0_pool_w_out,
        l1_norm, l1_sgu_w_in, l1_sgu_ln_g, l1_sgu_ln_b, l1_sgu_w_s, l1_sgu_b_s, l1_sgu_w_out,
        l2_norm, l2_dn_w_in, l2_dn_conv_w, l2_dn_a_log, l2_dn_dt_bias, l2_dn_o_gain, l2_dn_w_out,
        l3_norm, l3_pool_w_in, l3_pool_w_grp, l3_pool_scale, l3_pool_w_out,
        final_norm, tm_pool=512, tm_sgu=256, tm_gdn=256, nblk_state=8)
```

```python
import functools

import jax
import jax.numpy as jnp
from jax import lax
from jax.experimental import pallas as pl
from jax.experimental.pallas import tpu as pltpu

F32 = jnp.float32
BF16 = jnp.bfloat16
EPS = 1e-6
PAST_LEN = 16384
POOL_WINDOWS = (2, 4, 8, 16)
POOL_HIST = 15
POOL_HIST_PAD = 16
SGU_CHUNK = 128
SGU_GROUPS = 4
DN_HEADS = 8
DN_DK = 128
DN_DV = 128
DN_CONV = 4
CONV_HIST_PAD = 8
DN_CHUNK = 128
INV_BASE_BITS = 4
LANES = 128
VMEM_LIMIT = 60 * 1024 * 1024


def _rmsnorm(x, gain):
    ms = jnp.mean(x * x, axis=-1, keepdims=True)
    return x * lax.rsqrt(ms + EPS) * gain


def _silu(z):
    return z * jax.nn.sigmoid(z)


def _gelu(x):
    return 0.5 * x * (1.0 + lax.erf(x * (2.0 ** -0.5)))


def _dot(a, b):
    return jnp.dot(a, b, preferred_element_type=F32)


def _dot_nt(a, b):
    return lax.dot_general(a, b, (((1,), (1,)), ((), ())), preferred_element_type=F32)


def _dot_tn(a, b):
    return lax.dot_general(a, b, (((0,), (0,)), ((), ())), preferred_element_type=F32)


def _const_spec(shape):
    nd = len(shape)
    return pl.BlockSpec(shape, lambda *_: (0,) * nd, pipeline_mode=pl.Buffered(1))


def _params(n_grid):
    return pltpu.CompilerParams(
        dimension_semantics=("arbitrary",) * n_grid, vmem_limit_bytes=VMEM_LIMIT)


def _pool_tail(pooled_groups, z, x, wgrp_ref, scale_ref, wout_ref, fin_ref, final_norm):
    mixed = [_dot(p.astype(BF16), wgrp_ref[gi]) for gi, p in enumerate(pooled_groups)]
    mixed = jnp.concatenate(mixed, axis=1)
    gated = mixed * scale_ref[...] * _silu(z)
    y = _dot(gated.astype(BF16), wout_ref[...]) + x
    if final_norm:
        y = _rmsnorm(y, fin_ref[...])
    return y


def _pool_prompt_body(x_ref, gain_ref, win_ref, wgrp_ref, scale_ref, wout_ref, fin_ref,
                      y_ref, st_ref, hist_ref, *, tm, n_t, final_norm):
    t = pl.program_id(1)
    dp = scale_ref.shape[-1]
    gw = dp // len(POOL_WINDOWS)

    @pl.when(t == 0)
    def _():
        hist_ref[...] = jnp.zeros_like(hist_ref)

    x = x_ref[...]
    h = _rmsnorm(x, gain_ref[...]).astype(BF16)
    xz = _dot(h, win_ref[...])
    xb = xz[:, :dp]
    z = xz[:, dp:]
    ext = jnp.concatenate([hist_ref[...], xb], axis=0)
    last = xb[tm - POOL_HIST_PAD:, :]
    hist_ref[...] = last
    pos1 = lax.broadcasted_iota(jnp.int32, (tm, LANES), 0) + (t * tm + 1)
    pooled = []
    for gi, w in enumerate(POOL_WINDOWS):
        e = ext[:, gi * gw:(gi + 1) * gw]
        s = e + pltpu.roll(e, 1, 0)
        sh = 2
        while sh < w:
            s = s + pltpu.roll(s, sh, 0)
            sh *= 2
        s = s[POOL_HIST_PAD:, :]
        inv = 1.0 / jnp.minimum(pos1, w).astype(F32)
        inv = jnp.concatenate([inv] * (gw // LANES), axis=1)
        pooled.append(s * inv - xb[:, gi * gw:(gi + 1) * gw])
    y_ref[...] = _pool_tail(pooled, z, x, wgrp_ref, scale_ref, wout_ref, fin_ref, final_norm)

    @pl.when(t == n_t - 1)
    def _():
        st_ref[...] = last


def _pool_prompt(x, gain, win, wgrp, scale, wout, fin, *, tm, final_norm):
    b, t, d = x.shape
    dp = scale.shape[-1]
    n_t = t // tm
    body = functools.partial(_pool_prompt_body, tm=tm, n_t=n_t, final_norm=final_norm)
    return pl.pallas_call(
        body,
        grid=(b, n_t),
        in_specs=[
            pl.BlockSpec((None, tm, d), lambda i, j: (i, j, 0)),
            _const_spec(gain.shape), _const_spec(win.shape), _const_spec(wgrp.shape),
            _const_spec(scale.shape), _const_spec(wout.shape), _const_spec(fin.shape),
        ],
        out_specs=[
            pl.BlockSpec((None, tm, d), lambda i, j: (i, j, 0)),
            pl.BlockSpec((None, POOL_HIST_PAD, dp), lambda i, j: (i, 0, 0)),
        ],
        out_shape=[jax.ShapeDtypeStruct((b, t, d), F32),
                   jax.ShapeDtypeStruct((b, POOL_HIST_PAD, dp), F32)],
        scratch_shapes=[pltpu.VMEM((POOL_HIST_PAD, dp), F32)],
        compiler_params=_params(2),
        name="pool_prompt",
    )(x, gain, win, wgrp, scale, wout, fin)


def _pool_sample_body(x_ref, st_ref, gain_ref, win_ref, wgrp_ref, scale_ref, wout_ref, fin_ref,
                      y_ref, nst_ref, *, final_norm):
    dp = scale_ref.shape[-1]
    gw = dp // len(POOL_WINDOWS)
    x = x_ref[...]
    h = _rmsnorm(x, gain_ref[...]).astype(BF16)
    xz = _dot(h, win_ref[...])
    xb = xz[:, :dp]
    z = xz[:, dp:]
    pooled = []
    for gi, w in enumerate(POOL_WINDOWS):
        xg = xb[:, gi * gw:(gi + 1) * gw]
        s = xg
        for k in range(1, w):
            c0 = (POOL_HIST - k) * dp + gi * gw
            s = s + st_ref[:, c0:c0 + gw]
        cnt = float(min(PAST_LEN + 1, w))
        pooled.append(s * (1.0 / cnt) - xg)
    y_ref[...] = _pool_tail(pooled, z, x, wgrp_ref, scale_ref, wout_ref, fin_ref, final_norm)
    nst_ref[:, :(POOL_HIST - 1) * dp] = st_ref[:, dp:]
    nst_ref[:, (POOL_HIST - 1) * dp:] = xb


def _pool_sample(x, st, gain, win, wgrp, scale, wout, fin, *, final_norm):
    nb, d = x.shape
    body = functools.partial(_pool_sample_body, final_norm=final_norm)
    return pl.pallas_call(
        body,
        out_shape=[jax.ShapeDtypeStruct((nb, d), F32), jax.ShapeDtypeStruct(st.shape, F32)],
        compiler_params=_params(0),
        name="pool_sample",
    )(x, st, gain, win, wgrp, scale, wout, fin)


def _sgu_front(x, gain_ref, win_ref, lng_ref, lnb_ref):
    ds = lng_ref.shape[-1]
    h = _rmsnorm(x, gain_ref[...]).astype(BF16)
    uvz = _dot(h, win_ref[...])
    u = _gelu(uvz[:, :ds])
    v = _gelu(uvz[:, ds:2 * ds])
    z = uvz[:, 2 * ds:]
    mu = jnp.mean(v, axis=-1, keepdims=True)
    vc = v - mu
    var = jnp.mean(vc * vc, axis=-1, keepdims=True)
    vn = vc * lax.rsqrt(var + EPS) * lng_ref[...] + lnb_ref[...]
    return u, vn, z


def _sgu_prompt_body(x_ref, gain_ref, win_ref, lng_ref, lnb_ref, ws_ref, bsb_ref, wout_ref,
                     y_ref, *, tm):
    ds = lng_ref.shape[-1]
    gw = ds // SGU_GROUPS
    x = x_ref[...]
    u, vn, z = _sgu_front(x, gain_ref, win_ref, lng_ref, lnb_ref)
    ri = lax.broadcasted_iota(jnp.int32, (SGU_CHUNK, SGU_CHUNK), 0)
    ci = lax.broadcasted_iota(jnp.int32, (SGU_CHUNK, SGU_CHUNK), 1)
    ws = [jnp.where(ri >= ci, ws_ref[g], 0.0).astype(BF16) for g in range(SGU_GROUPS)]
    vnb = vn.astype(BF16)
    bsb = bsb_ref[...]
    rows = []
    for n in range(tm // SGU_CHUNK):
        r0 = n * SGU_CHUNK
        cols = [_dot(ws[g], vnb[r0:r0 + SGU_CHUNK, g * gw:(g + 1) * gw]) for g in range(SGU_GROUPS)]
        rows.append(jnp.concatenate(cols, axis=1) + bsb)
    s = jnp.concatenate(rows, axis=0)
    gated = u * s * _silu(z)
    y_ref[...] = _dot(gated.astype(BF16), wout_ref[...]) + x


def _sgu_prompt(x, gain, win, lng, lnb, ws, bsb, wout, *, tm):
    n, d = x.shape
    body = functools.partial(_sgu_prompt_body, tm=tm)
    return pl.pallas_call(
        body,
        grid=(n // tm,),
        in_specs=[pl.BlockSpec((tm, d), lambda i: (i, 0))] + [
            _const_spec(a.shape) for a in (gain, win, lng, lnb, ws, bsb, wout)],
        out_specs=pl.BlockSpec((tm, d), lambda i: (i, 0)),
        out_shape=jax.ShapeDtypeStruct((n, d), F32),
        compiler_params=_params(1),
        name="sgu_prompt",
    )(x, gain, win, lng, lnb, ws, bsb, wout)


def _sgu_sample_body(x_ref, gain_ref, win_ref, lng_ref, lnb_ref, w00_ref, b0_ref, wout_ref,
                     y_ref, v_ref):
    x = x_ref[...]
    u, vn, z = _sgu_front(x, gain_ref, win_ref, lng_ref, lnb_ref)
    s = vn * w00_ref[...] + b0_ref[...]
    gated = u * s * _silu(z)
    y_ref[...] = _dot(gated.astype(BF16), wout_ref[...]) + x
    v_ref[...] = vn


def _sgu_sample(x, gain, win, lng, lnb, w00, b0, wout):
    nb, d = x.shape
    ds = lng.shape[-1]
    return pl.pallas_call(
        _sgu_sample_body,
        out_shape=[jax.ShapeDtypeStruct((nb, d), F32), jax.ShapeDtypeStruct((nb, ds), F32)],
        compiler_params=_params(0),
        name="sgu_sample",
    )(x, gain, win, lng, lnb, w00, b0, wout)


def _gdn_gates(ab, alog_ref, dtb_ref):
    g = -jnp.exp(alog_ref[...]) * jax.nn.softplus(ab + dtb_ref[...])
    beta = jax.nn.sigmoid(ab)
    return g, beta


def _lane_pick(a, lane, idx):
    return jnp.sum(jnp.where(lane == idx, a, 0.0), axis=-1, keepdims=True)


def _gdn_heads(c, g_like, beta_all, emit):
    rows = c.shape[0]
    qk = DN_HEADS * DN_DK
    lane = lax.broadcasted_iota(jnp.int32, (rows, LANES), 1)
    for hh in range(DN_HEADS):
        qh = c[:, hh * DN_DK:(hh + 1) * DN_DK]
        kh = c[:, qk + hh * DN_DK:qk + (hh + 1) * DN_DK]
        vh = c[:, 2 * qk + hh * DN_DV:2 * qk + (hh + 1) * DN_DV]
        qn = qh * lax.rsqrt(jnp.sum(qh * qh, axis=-1, keepdims=True) + EPS) * (DN_DK ** -0.5)
        kn = kh * lax.rsqrt(jnp.sum(kh * kh, axis=-1, keepdims=True) + EPS)
        emit(hh, qn, kn, vh, _lane_pick(g_like, lane, hh), _lane_pick(beta_all, lane, DN_HEADS + hh))


def _gdn_proj_prompt_body(x_ref, gain_ref, win_ref, cw_ref, alog_ref, dtb_ref,
                          q_ref, k_ref, kb_ref, vb_ref, gam_ref, z_ref, cst_ref, hist_ref,
                          *, tm, n_t):
    t = pl.program_id(1)
    qk = DN_HEADS * DN_DK
    cch = 3 * qk

    @pl.when(t == 0)
    def _():
        hist_ref[...] = jnp.zeros_like(hist_ref)

    x = x_ref[...]
    h = _rmsnorm(x, gain_ref[...]).astype(BF16)
    proj = _dot(h, win_ref[...])
    qkv = proj[:, :cch]
    z_ref[...] = proj[:, cch:cch + qk]
    ab = proj[:, cch + qk:cch + qk + LANES]
    ext = jnp.concatenate([hist_ref[...], qkv], axis=0)
    last = qkv[tm - CONV_HIST_PAD:, :]
    hist_ref[...] = last
    cw = cw_ref[...]
    conv = ext * cw[DN_CONV - 1:DN_CONV, :]
    for j in range(1, DN_CONV):
        conv = conv + pltpu.roll(ext, j, 0) * cw[DN_CONV - 1 - j:DN_CONV - j, :]
    c = _silu(conv[CONV_HIST_PAD:, :])

    g, beta = _gdn_gates(ab, alog_ref, dtb_ref)
    rin = lax.broadcasted_iota(jnp.int32, (tm, LANES), 0) & (DN_CHUNK - 1)
    gam = g
    sh = 1
    while sh < DN_CHUNK:
        gam = gam + jnp.where(rin >= sh, pltpu.roll(gam, sh, 0), 0.0)
        sh *= 2

    def emit(hh, qn, kn, vh, g_col, b_col):
        sl = slice(hh * DN_DK, (hh + 1) * DN_DK)
        q_ref[:, sl] = qn.astype(BF16)
        k_ref[:, sl] = kn.astype(BF16)
        kb_ref[:, sl] = (b_col * kn).astype(BF16)
        vb_ref[:, sl] = (b_col * vh).astype(BF16)
        gam_ref[:, sl] = jnp.broadcast_to(g_col, (tm, DN_DK))

    _gdn_heads(c, gam, beta, emit)

    @pl.when(t == n_t - 1)
    def _():
        cst_ref[...] = last


def _gdn_proj_prompt(x, gain, win, cw, alog, dtb, *, tm):
    b, t, d = x.shape
    qk = DN_HEADS * DN_DK
    cch = 3 * qk
    n_t = t // tm
    body = functools.partial(_gdn_proj_prompt_body, tm=tm, n_t=n_t)
    tok = lambda i, j: (i, j, 0)
    return pl.pallas_call(
        body,
        grid=(b, n_t),
        in_specs=[pl.BlockSpec((None, tm, d), tok)] + [
            _const_spec(a.shape) for a in (gain, win, cw, alog, dtb)],
        out_specs=[pl.BlockSpec((None, tm, qk), tok)] * 6 + [
            pl.BlockSpec((None, CONV_HIST_PAD, cch), lambda i, j: (i, 0, 0))],
        out_shape=[jax.ShapeDtypeStruct((b, t, qk), BF16)] * 4 + [
            jax.ShapeDtypeStruct((b, t, qk), F32)] * 2 + [
            jax.ShapeDtypeStruct((b, CONV_HIST_PAD, cch), F32)],
        scratch_shapes=[pltpu.VMEM((CONV_HIST_PAD, cch), F32)],
        compiler_params=_params(2),
        name="gdn_proj_prompt",
    )(x, gain, win, cw, alog, dtb)


def _unit_lower_inverse_minus_eye(ms, in_base, at_level):
    bf = lambda xs: [x.astype(BF16) for x in xs]
    a = [jnp.where(in_base, m, 0.0) for m in ms]
    e = [-x for x in a]
    ab = bf(a)
    p = [_dot(x, x) for x in ab]
    for it in range(INV_BASE_BITS - 1):
        pb = bf(p)
        ep = [_dot(x, y) for x, y in zip(bf(e), pb)]
        e = [x + y + z for x, y, z in zip(e, p, ep)]
        if it < INV_BASE_BITS - 2:
            p = [_dot(x, x) for x in pb]
    for mask in at_level:
        low = [jnp.where(mask, m, 0.0) for m in ms]
        eb = bf(e)
        x = [lo + _dot(lb, y) for lo, lb, y in zip(low, bf(low), eb)]
        ex = [_dot(y, xb) for y, xb in zip(eb, bf(x))]
        e = [ei - (xi + yi) for ei, xi, yi in zip(e, x, ex)]
    return e


def _gdn_out(o_heads, z, x, og_ref, wout_ref):
    normed = [oh * lax.rsqrt(jnp.mean(oh * oh, axis=-1, keepdims=True) + EPS) * og_ref[...]
              for oh in o_heads]
    gated = jnp.concatenate(normed, axis=1) * _silu(z)
    return _dot(gated.astype(BF16), wout_ref[...]) + x


def _gdn_delta_prompt_body(q_ref, k_ref, kb_ref, vb_ref, gam_ref, z_ref, x_ref, og_ref, wout_ref,
                           y_ref, sfin_ref, s_ref, o_ref, *, tm, n_t):
    t = pl.program_id(1)

    @pl.when(t == 0)
    def _():
        s_ref[...] = jnp.zeros_like(s_ref)

    c = DN_CHUNK
    ri = lax.broadcasted_iota(jnp.int32, (c, c), 0)
    ci = lax.broadcasted_iota(jnp.int32, (c, c), 1)
    incl = ri >= ci
    strict = ri > ci
    top = ri ^ ci
    in_base = (top >> INV_BASE_BITS) == 0
    levels = range(INV_BASE_BITS, c.bit_length() - 1)
    at_level = [(top >> lv) == 1 for lv in levels]
    n_chunks = tm // c
    pairs = [(n, hh) for n in range(n_chunks) for hh in range(DN_HEADS)]

    def blk(ref, n, hh):
        return ref[n * c:(n + 1) * c, hh * DN_DK:(hh + 1) * DN_DK]

    kh = [blk(k_ref, *p) for p in pairs]
    kbh = [blk(kb_ref, *p) for p in pairs]
    gam = [blk(gam_ref, *p) for p in pairs]
    kk = [_dot_nt(a, b) for a, b in zip(kbh, kh)]
    dec = [jnp.exp(jnp.where(incl, g - g.T, -1e30)) for g in gam]
    m = [jnp.where(strict, a * d, 0.0) for a, d in zip(kk, dec)]
    e = _unit_lower_inverse_minus_eye(m, in_base, at_level)
    eg = [jnp.exp(g) for g in gam]
    rhs = [jnp.concatenate([blk(vb_ref, *p).astype(F32), a.astype(F32) * x], axis=1)
           for p, a, x in zip(pairs, kbh, eg)]
    uw = [r + _dot(x.astype(BF16), r.astype(BF16)) for r, x in zip(rhs, e)]
    qh = [blk(q_ref, *p) for p in pairs]
    qkm = [(_dot_nt(a, b) * d).astype(BF16) for a, b, d in zip(qh, kh, dec)]
    qd = [(a.astype(F32) * x).astype(BF16) for a, x in zip(qh, eg)]
    gl = [g[c - 1:c, :] for g in gam]
    kt = [(a.astype(F32) * jnp.exp(l - g)).astype(BF16) for a, l, g in zip(kh, gl, gam)]
    al = [jnp.exp(l) for l in gl]

    for n in range(n_chunks):
        idx = [n * DN_HEADS + hh for hh in range(DN_HEADS)]
        s_old = [s_ref[hh] for hh in range(DN_HEADS)]
        sb = [x.astype(BF16) for x in s_old]
        w = [uw[i][:, :DN_DV] - _dot(uw[i][:, DN_DV:].astype(BF16), y) for i, y in zip(idx, sb)]
        wb = [x.astype(BF16) for x in w]
        for hh, i in enumerate(idx):
            o_ref[n * c:(n + 1) * c, hh * DN_DV:(hh + 1) * DN_DV] = (
                _dot(qd[i], sb[hh]) + _dot(qkm[i], wb[hh]))
        for hh, i in enumerate(idx):
            s_ref[hh] = al[i] * s_old[hh] + _dot_tn(kt[i], wb[hh])
    o = o_ref[...]
    o_heads = [o[:, hh * DN_DV:(hh + 1) * DN_DV] for hh in range(DN_HEADS)]
    y_ref[...] = _gdn_out(o_heads, z_ref[...], x_ref[...], og_ref, wout_ref)

    @pl.when(t == n_t - 1)
    def _():
        sfin_ref[...] = s_ref[...]


def _gdn_delta_prompt(q, k, kb, vb, gam, z, x, og, wout, *, tm):
    b, t, d = x.shape
    qk = DN_HEADS * DN_DK
    n_t = t // tm
    body = functools.partial(_gdn_delta_prompt_body, tm=tm, n_t=n_t)
    tok = lambda i, j: (i, j, 0)
    return pl.pallas_call(
        body,
        grid=(b, n_t),
        in_specs=[pl.BlockSpec((None, tm, qk), tok)] * 6 + [pl.BlockSpec((None, tm, d), tok),
                  _const_spec(og.shape), _const_spec(wout.shape)],
        out_specs=[pl.BlockSpec((None, tm, d), tok),
                   pl.BlockSpec((None, DN_HEADS, DN_DK, DN_DV), lambda i, j: (i, 0, 0, 0))],
        out_shape=[jax.ShapeDtypeStruct((b, t, d), F32),
                   jax.ShapeDtypeStruct((b, DN_HEADS, DN_DK, DN_DV), F32)],
        scratch_shapes=[pltpu.VMEM((DN_HEADS, DN_DK, DN_DV), F32),
                        pltpu.VMEM((tm, DN_HEADS * DN_DV), F32)],
        compiler_params=_params(2),
        name="gdn_delta_prompt",
    )(q, k, kb, vb, gam, z, x, og, wout)


def _gdn_proj_sample_body(x_ref, cst_ref, gain_ref, win_ref, cw_ref, alog_ref, dtb_ref,
                          q_ref, k_ref, v_ref, eg_ref, beta_ref, z_ref, ncst_ref):
    qk = DN_HEADS * DN_DK
    cch = 3 * qk
    nb = x_ref.shape[0]
    x = x_ref[...]
    h = _rmsnorm(x, gain_ref[...]).astype(BF16)
    proj = _dot(h, win_ref[...])
    qkv = proj[:, :cch]
    z_ref[...] = proj[:, cch:cch + qk]
    ab = proj[:, cch + qk:cch + qk + LANES]
    cw = cw_ref[...]
    conv = qkv * cw[DN_CONV - 1:DN_CONV, :]
    for j in range(DN_CONV - 1):
        conv = conv + cst_ref[:, j * cch:(j + 1) * cch] * cw[j:j + 1, :]
    c = _silu(conv)
    ncst_ref[:, :(DN_CONV - 2) * cch] = cst_ref[:, cch:]
    ncst_ref[:, (DN_CONV - 2) * cch:] = qkv
    g, beta = _gdn_gates(ab, alog_ref, dtb_ref)

    def emit(hh, qn, kn, vh, g_col, b_col):
        sl = slice(hh * DN_DK, (hh + 1) * DN_DK)
        q_ref[:, sl] = qn
        k_ref[:, sl] = kn
        v_ref[:, sl] = vh
        eg_ref[:, sl] = jnp.broadcast_to(jnp.exp(g_col), (nb, DN_DK))
        beta_ref[:, sl] = jnp.broadcast_to(b_col, (nb, DN_DK))

    _gdn_heads(c, g, beta, emit)


def _gdn_proj_sample(x, cst, gain, win, cw, alog, dtb):
    nb, d = x.shape
    qk = DN_HEADS * DN_DK
    return pl.pallas_call(
        _gdn_proj_sample_body,
        out_shape=[jax.ShapeDtypeStruct((nb, qk), F32)] * 6 + [jax.ShapeDtypeStruct(cst.shape, F32)],
        compiler_params=_params(0),
        name="gdn_proj_sample",
    )(x, cst, gain, win, cw, alog, dtb)


def _gdn_state_sample_body(q_ref, k_ref, v_ref, eg_ref, beta_ref, s_ref, o_ref, ns_ref, *, nblk):
    hk = DN_HEADS * DN_DK
    row = lax.broadcasted_iota(jnp.int32, (DN_HEADS, hk), 0)
    blk = lax.broadcasted_iota(jnp.int32, (DN_HEADS, hk), 1) // DN_DK
    own = row == blk

    def blockdiag(a):
        return jnp.where(own, jnp.concatenate([a] * DN_HEADS, axis=1), 0.0).astype(BF16)

    for r in range(nblk):
        s_old = s_ref[r]
        kbd = blockdiag(k_ref[r])
        eg = eg_ref[r]
        kts = _dot(kbd, s_old.reshape(hk, DN_DV).astype(BF16))
        w = beta_ref[r] * (v_ref[r] - eg * kts)
        outer = _dot_tn(kbd, w.astype(BF16))
        s_new = s_old * eg[:, None, :] + outer.reshape(DN_HEADS, DN_DK, DN_DV)
        ns_ref[r] = s_new
        o_ref[r] = _dot(blockdiag(q_ref[r]), s_new.reshape(hk, DN_DV).astype(BF16))


def _gdn_state_sample(q, k, v, eg, beta, s, *, nblk):
    nb = s.shape[0]
    body = functools.partial(_gdn_state_sample_body, nblk=nblk)
    vec = pl.BlockSpec((nblk, DN_HEADS, DN_DK), lambda i: (i, 0, 0))
    st = pl.BlockSpec((nblk, DN_HEADS, DN_DK, DN_DV), lambda i: (i, 0, 0, 0))
    return pl.pallas_call(
        body,
        grid=(nb // nblk,),
        in_specs=[vec] * 5 + [st],
        out_specs=[vec, st],
        out_shape=[jax.ShapeDtypeStruct((nb, DN_HEADS, DN_DV), F32),
                   jax.ShapeDtypeStruct(s.shape, F32)],
        compiler_params=_params(1),
        name="gdn_state_sample",
    )(q, k, v, eg, beta, s)


def _gdn_out_sample_body(o_ref, z_ref, x_ref, og_ref, wout_ref, y_ref):
    o = o_ref[...]
    o_heads = [o[:, hh * DN_DV:(hh + 1) * DN_DV] for hh in range(DN_HEADS)]
    y_ref[...] = _gdn_out(o_heads, z_ref[...], x_ref[...], og_ref, wout_ref)


def _gdn_out_sample(o, z, x, og, wout):
    return pl.pallas_call(
        _gdn_out_sample_body,
        out_shape=jax.ShapeDtypeStruct(x.shape, F32),
        compiler_params=_params(0),
        name="gdn_out_sample",
    )(o, z, x, og, wout)


def _row(a):
    return a.reshape(1, -1).astype(F32)


def _forward(x_prompt, x_sample, state_pool_l0, state_conv_l2, state_delta_l2, state_pool_l3,
             l0_norm, l0_pool_w_in, l0_pool_w_grp, l0_pool_scale, l0_pool_w_out,
             l1_norm, l1_sgu_w_in, l1_sgu_ln_g, l1_sgu_ln_b, l1_sgu_w_s, l1_sgu_b_s, l1_sgu_w_out,
             l2_norm, l2_dn_w_in, l2_dn_conv_w, l2_dn_a_log, l2_dn_dt_bias, l2_dn_o_gain, l2_dn_w_out,
             l3_norm, l3_pool_w_in, l3_pool_w_grp, l3_pool_scale, l3_pool_w_out,
             final_norm, *, tm_pool, tm_sgu, tm_gdn, nblk_state):
    b, t, d = x_prompt.shape
    nb = x_sample.shape[0]
    dp = l0_pool_scale.shape[0]
    ds = l1_sgu_ln_g.shape[0]
    gw_s = ds // SGU_GROUPS
    qk = DN_HEADS * DN_DK
    cch = 3 * qk
    fin = _row(final_norm)
    xs = x_sample.reshape(nb, d)

    def pool_layer(xp, xs_, st, norm, w_in, w_grp, scale, w_out, final):
        args = (_row(norm), w_in.astype(BF16), w_grp.astype(BF16), _row(scale), w_out.astype(BF16), fin)
        yp, st_p = _pool_prompt(xp, *args, tm=tm_pool, final_norm=final)
        ys, st_s = _pool_sample(xs_, st.reshape(nb, POOL_HIST * dp), *args, final_norm=final)
        return yp, ys, st_p[:, POOL_HIST_PAD - POOL_HIST:, :], st_s.reshape(nb, POOL_HIST, dp)

    xp, xs, pool0_p, pool0_s = pool_layer(
        x_prompt, xs, state_pool_l0, l0_norm, l0_pool_w_in, l0_pool_w_grp, l0_pool_scale,
        l0_pool_w_out, False)

    sgu_w_in = l1_sgu_w_in.astype(BF16)
    sgu_w_out = l1_sgu_w_out.astype(BF16)
    bsb = jnp.repeat(l1_sgu_b_s.T.astype(F32), gw_s, axis=1)
    xp = _sgu_prompt(xp.reshape(b * t, d), _row(l1_norm), sgu_w_in, _row(l1_sgu_ln_g),
                     _row(l1_sgu_ln_b), l1_sgu_w_s.astype(F32), bsb, sgu_w_out,
                     tm=tm_sgu).reshape(b, t, d)
    w00 = jnp.repeat(l1_sgu_w_s[:, 0, 0].astype(F32), gw_s)[None, :]
    b0 = jnp.repeat(l1_sgu_b_s[:, 0].astype(F32), gw_s)[None, :]
    xs, sgu1_s = _sgu_sample(xs, _row(l1_norm), sgu_w_in, _row(l1_sgu_ln_g), _row(l1_sgu_ln_b),
                             w00, b0, sgu_w_out)

    n_proj = l2_dn_w_in.shape[1]
    n_pad = cch + qk + LANES
    dn_w_in = jnp.pad(l2_dn_w_in, ((0, 0), (0, n_pad - n_proj))).astype(BF16)
    dn_w_out = l2_dn_w_out.astype(BF16)
    alog = jnp.pad(l2_dn_a_log.astype(F32), (0, LANES - DN_HEADS))[None, :]
    dtb = jnp.pad(l2_dn_dt_bias.astype(F32), (0, LANES - DN_HEADS))[None, :]
    cw = l2_dn_conv_w.astype(F32)
    og = _row(l2_dn_o_gain)
    q, k, kb, vb, gam, z, conv2_p = _gdn_proj_prompt(xp, _row(l2_norm), dn_w_in, cw, alog, dtb,
                                                     tm=tm_gdn)
    xp, dn2_p = _gdn_delta_prompt(q, k, kb, vb, gam, z, xp, og, dn_w_out, tm=tm_gdn)
    conv2_p = conv2_p[:, CONV_HIST_PAD - (DN_CONV - 1):, :]

    qs, ks, vs, egs, betas, zs, conv2_s = _gdn_proj_sample(
        xs, state_conv_l2.reshape(nb, (DN_CONV - 1) * cch), _row(l2_norm), dn_w_in, cw, alog, dtb)
    hv = lambda a: a.reshape(nb, DN_HEADS, DN_DK)
    o_s, dn2_s = _gdn_state_sample(hv(qs), hv(ks), hv(vs), hv(egs), hv(betas), state_delta_l2,
                                   nblk=nblk_state)
    xs = _gdn_out_sample(o_s.reshape(nb, qk), zs, xs, og, dn_w_out)
    conv2_s = conv2_s.reshape(nb, DN_CONV - 1, cch)

    yp, ys, pool3_p, pool3_s = pool_layer(
        xp, xs, state_pool_l3, l3_norm, l3_pool_w_in, l3_pool_w_grp, l3_pool_scale,
        l3_pool_w_out, True)

    return (yp, ys.reshape(nb, 1, d), pool0_p, pool0_s, sgu1_s.reshape(nb, 1, ds),
            conv2_p, conv2_s, dn2_p, dn2_s, pool3_p, pool3_s)


def kernel(x_prompt, x_sample, state_pool_l0, state_conv_l2, state_delta_l2, state_pool_l3, l0_norm, l0_pool_w_in, l0_pool_w_grp, l0_pool_scale, l0_pool_w_out, l1_norm, l1_sgu_w_in, l1_sgu_ln_g, l1_sgu_ln_b, l1_sgu_w_s, l1_sgu_b_s, l1_sgu_w_out, l2_norm, l2_dn_w_in, l2_dn_conv_w, l2_dn_a_log, l2_dn_dt_bias, l2_dn_o_gain, l2_dn_w_out, l3_norm, l3_pool_w_in, l3_pool_w_grp, l3_pool_scale, l3_pool_w_out, final_norm):
    return _forward(
        x_prompt, x_sample, state_pool_l0, state_conv_l2, state_delta_l2, state_pool_l3,
        l0_norm, l0_pool_w_in, l0_pool_w_grp, l0_pool_scale, l0_pool_w_out,
        l1_norm, l1_sgu_w_in, l1_sgu_ln_g, l1_sgu_ln_b, l1_sgu_w_s, l1_sgu_b_s, l1_sgu_w_out,
        l2_norm, l2_dn_w_in, l2_dn_conv_w, l2_dn_a_log, l2_dn_dt_bias, l2_dn_o_gain, l2_dn_w_out,
        l3_norm, l3_pool_w_in, l3_pool_w_grp, l3_pool_scale, l3_pool_w_out,
        final_norm, tm_pool=512, tm_sgu=256, tm_gdn=256, nblk_state=8)
```

```python
import functools

import jax
import jax.numpy as jnp
from jax import lax
from jax.experimental import pallas as pl
from jax.experimental.pallas import tpu as pltpu

F32 = jnp.float32
BF16 = jnp.bfloat16
EPS = 1e-6
PAST_LEN = 16384
POOL_WINDOWS = (2, 4, 8, 16)
POOL_HIST = 15
POOL_HIST_PAD = 16
SGU_CHUNK = 128
SGU_GROUPS = 4
DN_HEADS = 8
DN_DK = 128
DN_DV = 128
DN_CONV = 4
CONV_HIST_PAD = 8
DN_CHUNK = 128
INV_BASE_BITS = 4
LANES = 128
VMEM_LIMIT = 60 * 1024 * 1024


def _rmsnorm(x, gain):
    ms = jnp.mean(x * x, axis=-1, keepdims=True)
    return x * lax.rsqrt(ms + EPS) * gain


def _silu(z):
    return z * jax.nn.sigmoid(z)


def _gelu(x):
    return 0.5 * x * (1.0 + lax.erf(x * (2.0 ** -0.5)))


def _dot(a, b):
    return jnp.dot(a, b, preferred_element_type=F32)


def _dot_nt(a, b):
    return lax.dot_general(a, b, (((1,), (1,)), ((), ())), preferred_element_type=F32)


def _dot_tn(a, b):
    return lax.dot_general(a, b, (((0,), (0,)), ((), ())), preferred_element_type=F32)


def _load_rows(ref):
    return ref[:, 0, :] if len(ref.shape) == 3 else ref[...]


def _store_rows(ref, val):
    if len(ref.shape) == 3:
        ref[:, 0, :] = val
    else:
        ref[...] = val


def _const_spec(shape):
    nd = len(shape)
    return pl.BlockSpec(shape, lambda *_: (0,) * nd, pipeline_mode=pl.Buffered(1))


def _params(n_grid):
    return pltpu.CompilerParams(
        dimension_semantics=("arbitrary",) * n_grid, vmem_limit_bytes=VMEM_LIMIT)


def _pool_tail(pooled_groups, z, x, wgrp_ref, scale_ref, wout_ref, fin_ref, final_norm):
    mixed = [_dot(p.astype(BF16), wgrp_ref[gi]) for gi, p in enumerate(pooled_groups)]
    mixed = jnp.concatenate(mixed, axis=1)
    gated = mixed * scale_ref[...] * _silu(z)
    y = _dot(gated.astype(BF16), wout_ref[...]) + x
    if final_norm:
        y = _rmsnorm(y, fin_ref[...])
    return y


def _pool_prompt_body(x_ref, gain_ref, win_ref, wgrp_ref, scale_ref, wout_ref, fin_ref,
                      y_ref, st_ref, hist_ref, *, tm, n_t, final_norm):
    t = pl.program_id(1)
    dp = scale_ref.shape[-1]
    gw = dp // len(POOL_WINDOWS)

    @pl.when(t == 0)
    def _():
        hist_ref[...] = jnp.zeros_like(hist_ref)

    x = x_ref[...]
    h = _rmsnorm(x, gain_ref[...]).astype(BF16)
    xz = _dot(h, win_ref[...])
    xb = xz[:, :dp]
    z = xz[:, dp:]
    ext = jnp.concatenate([hist_ref[...], xb], axis=0)
    last = xb[tm - POOL_HIST_PAD:, :]
    hist_ref[...] = last
    pos1 = lax.broadcasted_iota(jnp.int32, (tm, LANES), 0) + (t * tm + 1)
    pooled = []
    for gi, w in enumerate(POOL_WINDOWS):
        e = ext[:, gi * gw:(gi + 1) * gw]
        s = e + pltpu.roll(e, 1, 0)
        sh = 2
        while sh < w:
            s = s + pltpu.roll(s, sh, 0)
            sh *= 2
        s = s[POOL_HIST_PAD:, :]
        inv = 1.0 / jnp.minimum(pos1, w).astype(F32)
        inv = jnp.concatenate([inv] * (gw // LANES), axis=1)
        pooled.append(s * inv - xb[:, gi * gw:(gi + 1) * gw])
    y_ref[...] = _pool_tail(pooled, z, x, wgrp_ref, scale_ref, wout_ref, fin_ref, final_norm)

    @pl.when(t == n_t - 1)
    def _():
        st_ref[...] = last


def _pool_prompt(x, gain, win, wgrp, scale, wout, fin, *, tm, final_norm):
    b, t, d = x.shape
    dp = scale.shape[-1]
    n_t = t // tm
    body = functools.partial(_pool_prompt_body, tm=tm, n_t=n_t, final_norm=final_norm)
    return pl.pallas_call(
        body,
        grid=(b, n_t),
        in_specs=[
            pl.BlockSpec((None, tm, d), lambda i, j: (i, j, 0)),
            _const_spec(gain.shape), _const_spec(win.shape), _const_spec(wgrp.shape),
            _const_spec(scale.shape), _const_spec(wout.shape), _const_spec(fin.shape),
        ],
        out_specs=[
            pl.BlockSpec((None, tm, d), lambda i, j: (i, j, 0)),
            pl.BlockSpec((None, POOL_HIST_PAD, dp), lambda i, j: (i, 0, 0)),
        ],
        out_shape=[jax.ShapeDtypeStruct((b, t, d), F32),
                   jax.ShapeDtypeStruct((b, POOL_HIST_PAD, dp), F32)],
        scratch_shapes=[pltpu.VMEM((POOL_HIST_PAD, dp), F32)],
        compiler_params=_params(2),
        name="pool_prompt",
    )(x, gain, win, wgrp, scale, wout, fin)


def _pool_sample_body(x_ref, st_ref, gain_ref, win_ref, wgrp_ref, scale_ref, wout_ref, fin_ref,
                      y_ref, nst_ref, *, final_norm):
    dp = scale_ref.shape[-1]
    gw = dp // len(POOL_WINDOWS)
    x = _load_rows(x_ref)
    h = _rmsnorm(x, gain_ref[...]).astype(BF16)
    xz = _dot(h, win_ref[...])
    xb = xz[:, :dp]
    z = xz[:, dp:]
    pooled = []
    for gi, w in enumerate(POOL_WINDOWS):
        xg = xb[:, gi * gw:(gi + 1) * gw]
        s = xg
        for k in range(1, w):
            s = s + st_ref[:, POOL_HIST - k, gi * gw:(gi + 1) * gw]
        cnt = float(min(PAST_LEN + 1, w))
        pooled.append(s * (1.0 / cnt) - xg)
    _store_rows(y_ref, _pool_tail(pooled, z, x, wgrp_ref, scale_ref, wout_ref, fin_ref, final_norm))
    nst_ref[:, :POOL_HIST - 1, :] = st_ref[:, 1:, :]
    nst_ref[:, POOL_HIST - 1, :] = xb


def _pool_sample(x, st, gain, win, wgrp, scale, wout, fin, *, final_norm, y_shape):
    body = functools.partial(_pool_sample_body, final_norm=final_norm)
    return pl.pallas_call(
        body,
        out_shape=[jax.ShapeDtypeStruct(y_shape, F32), jax.ShapeDtypeStruct(st.shape, F32)],
        compiler_params=_params(0),
        name="pool_sample",
    )(x, st, gain, win, wgrp, scale, wout, fin)


def _sgu_front(x, gain_ref, win_ref, lng_ref, lnb_ref):
    ds = lng_ref.shape[-1]
    h = _rmsnorm(x, gain_ref[...]).astype(BF16)
    uvz = _dot(h, win_ref[...])
    u = _gelu(uvz[:, :ds])
    v = _gelu(uvz[:, ds:2 * ds])
    z = uvz[:, 2 * ds:]
    mu = jnp.mean(v, axis=-1, keepdims=True)
    vc = v - mu
    var = jnp.mean(vc * vc, axis=-1, keepdims=True)
    vn = vc * lax.rsqrt(var + EPS) * lng_ref[...] + lnb_ref[...]
    return u, vn, z


def _sgu_prompt_body(x_ref, gain_ref, win_ref, lng_ref, lnb_ref, ws_ref, bsb_ref, wout_ref,
                     y_ref, *, tm):
    ds = lng_ref.shape[-1]
    gw = ds // SGU_GROUPS
    x = x_ref[...]
    u, vn, z = _sgu_front(x, gain_ref, win_ref, lng_ref, lnb_ref)
    ri = lax.broadcasted_iota(jnp.int32, (SGU_CHUNK, SGU_CHUNK), 0)
    ci = lax.broadcasted_iota(jnp.int32, (SGU_CHUNK, SGU_CHUNK), 1)
    ws = [jnp.where(ri >= ci, ws_ref[g], 0.0).astype(BF16) for g in range(SGU_GROUPS)]
    vnb = vn.astype(BF16)
    bsb = bsb_ref[...]
    rows = []
    for n in range(tm // SGU_CHUNK):
        r0 = n * SGU_CHUNK
        cols = [_dot(ws[g], vnb[r0:r0 + SGU_CHUNK, g * gw:(g + 1) * gw]) for g in range(SGU_GROUPS)]
        rows.append(jnp.concatenate(cols, axis=1) + bsb)
    s = jnp.concatenate(rows, axis=0)
    gated = u * s * _silu(z)
    y_ref[...] = _dot(gated.astype(BF16), wout_ref[...]) + x


def _sgu_prompt(x, gain, win, lng, lnb, ws, bsb, wout, *, tm):
    n, d = x.shape
    body = functools.partial(_sgu_prompt_body, tm=tm)
    return pl.pallas_call(
        body,
        grid=(n // tm,),
        in_specs=[pl.BlockSpec((tm, d), lambda i: (i, 0))] + [
            _const_spec(a.shape) for a in (gain, win, lng, lnb, ws, bsb, wout)],
        out_specs=pl.BlockSpec((tm, d), lambda i: (i, 0)),
        out_shape=jax.ShapeDtypeStruct((n, d), F32),
        compiler_params=_params(1),
        name="sgu_prompt",
    )(x, gain, win, lng, lnb, ws, bsb, wout)


def _sgu_sample_body(x_ref, gain_ref, win_ref, lng_ref, lnb_ref, w00_ref, b0_ref, wout_ref,
                     y_ref, v_ref):
    x = x_ref[...]
    u, vn, z = _sgu_front(x, gain_ref, win_ref, lng_ref, lnb_ref)
    s = vn * w00_ref[...] + b0_ref[...]
    gated = u * s * _silu(z)
    y_ref[...] = _dot(gated.astype(BF16), wout_ref[...]) + x
    _store_rows(v_ref, vn)


def _sgu_sample(x, gain, win, lng, lnb, w00, b0, wout):
    nb, d = x.shape
    ds = lng.shape[-1]
    return pl.pallas_call(
        _sgu_sample_body,
        out_shape=[jax.ShapeDtypeStruct((nb, d), F32), jax.ShapeDtypeStruct((nb, 1, ds), F32)],
        compiler_params=_params(0),
        name="sgu_sample",
    )(x, gain, win, lng, lnb, w00, b0, wout)


def _gdn_gates(ab, alog_ref, dtb_ref):
    g = -jnp.exp(alog_ref[...]) * jax.nn.softplus(ab + dtb_ref[...])
    beta = jax.nn.sigmoid(ab)
    return g, beta


def _lane_pick(a, lane, idx):
    return jnp.sum(jnp.where(lane == idx, a, 0.0), axis=-1, keepdims=True)


def _gdn_heads(c, g_like, beta_all, emit):
    rows = c.shape[0]
    qk = DN_HEADS * DN_DK
    lane = lax.broadcasted_iota(jnp.int32, (rows, LANES), 1)
    for hh in range(DN_HEADS):
        qh = c[:, hh * DN_DK:(hh + 1) * DN_DK]
        kh = c[:, qk + hh * DN_DK:qk + (hh + 1) * DN_DK]
        vh = c[:, 2 * qk + hh * DN_DV:2 * qk + (hh + 1) * DN_DV]
        qn = qh * lax.rsqrt(jnp.sum(qh * qh, axis=-1, keepdims=True) + EPS) * (DN_DK ** -0.5)
        kn = kh * lax.rsqrt(jnp.sum(kh * kh, axis=-1, keepdims=True) + EPS)
        emit(hh, qn, kn, vh, _lane_pick(g_like, lane, hh), _lane_pick(beta_all, lane, DN_HEADS + hh))


def _gdn_proj_prompt_body(x_ref, gain_ref, win_ref, cw_ref, alog_ref, dtb_ref,
                          q_ref, k_ref, kb_ref, vb_ref, gam_ref, z_ref, cst_ref, hist_ref,
                          *, tm, n_t):
    t = pl.program_id(1)
    qk = DN_HEADS * DN_DK
    cch = 3 * qk

    @pl.when(t == 0)
    def _():
        hist_ref[...] = jnp.zeros_like(hist_ref)

    x = x_ref[...]
    h = _rmsnorm(x, gain_ref[...]).astype(BF16)
    proj = _dot(h, win_ref[...])
    qkv = proj[:, :cch]
    z_ref[...] = proj[:, cch:cch + qk]
    ab = proj[:, cch + qk:cch + qk + LANES]
    ext = jnp.concatenate([hist_ref[...], qkv], axis=0)
    last = qkv[tm - CONV_HIST_PAD:, :]
    hist_ref[...] = last
    cw = cw_ref[...]
    conv = ext * cw[DN_CONV - 1:DN_CONV, :]
    for j in range(1, DN_CONV):
        conv = conv + pltpu.roll(ext, j, 0) * cw[DN_CONV - 1 - j:DN_CONV - j, :]
    c = _silu(conv[CONV_HIST_PAD:, :])

    g, beta = _gdn_gates(ab, alog_ref, dtb_ref)
    rin = lax.broadcasted_iota(jnp.int32, (tm, LANES), 0) & (DN_CHUNK - 1)
    gam = g
    sh = 1
    while sh < DN_CHUNK:
        gam = gam + jnp.where(rin >= sh, pltpu.roll(gam, sh, 0), 0.0)
        sh *= 2

    def emit(hh, qn, kn, vh, g_col, b_col):
        sl = slice(hh * DN_DK, (hh + 1) * DN_DK)
        q_ref[:, sl] = qn.astype(BF16)
        k_ref[:, sl] = kn.astype(BF16)
        kb_ref[:, sl] = (b_col * kn).astype(BF16)
        vb_ref[:, sl] = (b_col * vh).astype(BF16)
        gam_ref[:, sl] = jnp.broadcast_to(g_col, (tm, DN_DK))

    _gdn_heads(c, gam, beta, emit)

    @pl.when(t == n_t - 1)
    def _():
        cst_ref[...] = last


def _gdn_proj_prompt(x, gain, win, cw, alog, dtb, *, tm):
    b, t, d = x.shape
    qk = DN_HEADS * DN_DK
    cch = 3 * qk
    n_t = t // tm
    body = functools.partial(_gdn_proj_prompt_body, tm=tm, n_t=n_t)
    tok = lambda i, j: (i, j, 0)
    return pl.pallas_call(
        body,
        grid=(b, n_t),
        in_specs=[pl.BlockSpec((None, tm, d), tok)] + [
            _const_spec(a.shape) for a in (gain, win, cw, alog, dtb)],
        out_specs=[pl.BlockSpec((None, tm, qk), tok)] * 6 + [
            pl.BlockSpec((None, CONV_HIST_PAD, cch), lambda i, j: (i, 0, 0))],
        out_shape=[jax.ShapeDtypeStruct((b, t, qk), BF16)] * 4 + [
            jax.ShapeDtypeStruct((b, t, qk), F32)] * 2 + [
            jax.ShapeDtypeStruct((b, CONV_HIST_PAD, cch), F32)],
        scratch_shapes=[pltpu.VMEM((CONV_HIST_PAD, cch), F32)],
        compiler_params=_params(2),
        name="gdn_proj_prompt",
    )(x, gain, win, cw, alog, dtb)


def _unit_lower_inverse_minus_eye(ms, in_base, at_level):
    bf = lambda xs: [x.astype(BF16) for x in xs]
    a = [jnp.where(in_base, m, 0.0) for m in ms]
    e = [-x for x in a]
    ab = bf(a)
    p = [_dot(x, x) for x in ab]
    for it in range(INV_BASE_BITS - 1):
        pb = bf(p)
        ep = [_dot(x, y) for x, y in zip(bf(e), pb)]
        e = [x + y + z for x, y, z in zip(e, p, ep)]
        if it < INV_BASE_BITS - 2:
            p = [_dot(x, x) for x in pb]
    for mask in at_level:
        low = [jnp.where(mask, m, 0.0) for m in ms]
        eb = bf(e)
        x = [lo + _dot(lb, y) for lo, lb, y in zip(low, bf(low), eb)]
        ex = [_dot(y, xb) for y, xb in zip(eb, bf(x))]
        e = [ei - (xi + yi) for ei, xi, yi in zip(e, x, ex)]
    return e


def _gdn_out(o_heads, z, x, og_ref, wout_ref):
    normed = [oh * lax.rsqrt(jnp.mean(oh * oh, axis=-1, keepdims=True) + EPS) * og_ref[...]
              for oh in o_heads]
    gated = jnp.concatenate(normed, axis=1) * _silu(z)
    return _dot(gated.astype(BF16), wout_ref[...]) + x


def _gdn_delta_prompt_body(q_ref, k_ref, kb_ref, vb_ref, gam_ref, z_ref, x_ref, og_ref, wout_ref,
                           y_ref, sfin_ref, s_ref, o_ref, *, tm, n_t):
    t = pl.program_id(1)

    @pl.when(t == 0)
    def _():
        s_ref[...] = jnp.zeros_like(s_ref)

    c = DN_CHUNK
    ri = lax.broadcasted_iota(jnp.int32, (c, c), 0)
    ci = lax.broadcasted_iota(jnp.int32, (c, c), 1)
    incl = ri >= ci
    strict = ri > ci
    top = ri ^ ci
    in_base = (top >> INV_BASE_BITS) == 0
    levels = range(INV_BASE_BITS, c.bit_length() - 1)
    at_level = [(top >> lv) == 1 for lv in levels]
    n_chunks = tm // c
    pairs = [(n, hh) for n in range(n_chunks) for hh in range(DN_HEADS)]

    def blk(ref, n, hh):
        return ref[n * c:(n + 1) * c, hh * DN_DK:(hh + 1) * DN_DK]

    kh = [blk(k_ref, *p) for p in pairs]
    kbh = [blk(kb_ref, *p) for p in pairs]
    gam = [blk(gam_ref, *p) for p in pairs]
    kk = [_dot_nt(a, b) for a, b in zip(kbh, kh)]
    dec = [jnp.exp(jnp.where(incl, g - g.T, -1e30)) for g in gam]
    m = [jnp.where(strict, a * d, 0.0) for a, d in zip(kk, dec)]
    e = _unit_lower_inverse_minus_eye(m, in_base, at_level)
    eg = [jnp.exp(g) for g in gam]
    rhs = [jnp.concatenate([blk(vb_ref, *p).astype(F32), a.astype(F32) * x], axis=1)
           for p, a, x in zip(pairs, kbh, eg)]
    uw = [r + _dot(x.astype(BF16), r.astype(BF16)) for r, x in zip(rhs, e)]
    qh = [blk(q_ref, *p) for p in pairs]
    qkm = [(_dot_nt(a, b) * d).astype(BF16) for a, b, d in zip(qh, kh, dec)]
    qd = [(a.astype(F32) * x).astype(BF16) for a, x in zip(qh, eg)]
    gl = [g[c - 1:c, :] for g in gam]
    kt = [(a.astype(F32) * jnp.exp(l - g)).astype(BF16) for a, l, g in zip(kh, gl, gam)]
    al = [jnp.exp(l) for l in gl]

    for n in range(n_chunks):
        idx = [n * DN_HEADS + hh for hh in range(DN_HEADS)]
        s_old = [s_ref[hh] for hh in range(DN_HEADS)]
        sb = [x.astype(BF16) for x in s_old]
        w = [uw[i][:, :DN_DV] - _dot(uw[i][:, DN_DV:].astype(BF16), y) for i, y in zip(idx, sb)]
        wb = [x.astype(BF16) for x in w]
        for hh, i in enumerate(idx):
            o_ref[n * c:(n + 1) * c, hh * DN_DV:(hh + 1) * DN_DV] = (
                _dot(qd[i], sb[hh]) + _dot(qkm[i], wb[hh]))
        for hh, i in enumerate(idx):
            s_ref[hh] = al[i] * s_old[hh] + _dot_tn(kt[i], wb[hh])
    o = o_ref[...]
    o_heads = [o[:, hh * DN_DV:(hh + 1) * DN_DV] for hh in range(DN_HEADS)]
    y_ref[...] = _gdn_out(o_heads, z_ref[...], x_ref[...], og_ref, wout_ref)

    @pl.when(t == n_t - 1)
    def _():
        sfin_ref[...] = s_ref[...]


def _gdn_delta_prompt(q, k, kb, vb, gam, z, x, og, wout, *, tm):
    b, t, d = x.shape
    qk = DN_HEADS * DN_DK
    n_t = t // tm
    body = functools.partial(_gdn_delta_prompt_body, tm=tm, n_t=n_t)
    tok = lambda i, j: (i, j, 0)
    return pl.pallas_call(
        body,
        grid=(b, n_t),
        in_specs=[pl.BlockSpec((None, tm, qk), tok)] * 6 + [pl.BlockSpec((None, tm, d), tok),
                  _const_spec(og.shape), _const_spec(wout.shape)],
        out_specs=[pl.BlockSpec((None, tm, d), tok),
                   pl.BlockSpec((None, DN_HEADS, DN_DK, DN_DV), lambda i, j: (i, 0, 0, 0))],
        out_shape=[jax.ShapeDtypeStruct((b, t, d), F32),
                   jax.ShapeDtypeStruct((b, DN_HEADS, DN_DK, DN_DV), F32)],
        scratch_shapes=[pltpu.VMEM((DN_HEADS, DN_DK, DN_DV), F32),
                        pltpu.VMEM((tm, DN_HEADS * DN_DV), F32)],
        compiler_params=_params(2),
        name="gdn_delta_prompt",
    )(q, k, kb, vb, gam, z, x, og, wout)


def _gdn_proj_sample_body(x_ref, cst_ref, gain_ref, win_ref, cw_ref, alog_ref, dtb_ref,
                          q_ref, k_ref, v_ref, eg_ref, beta_ref, z_ref, ncst_ref):
    qk = DN_HEADS * DN_DK
    cch = 3 * qk
    nb = x_ref.shape[0]
    x = x_ref[...]
    h = _rmsnorm(x, gain_ref[...]).astype(BF16)
    proj = _dot(h, win_ref[...])
    qkv = proj[:, :cch]
    z_ref[...] = proj[:, cch:cch + qk]
    ab = proj[:, cch + qk:cch + qk + LANES]
    cw = cw_ref[...]
    conv = qkv * cw[DN_CONV - 1:DN_CONV, :]
    for j in range(DN_CONV - 1):
        conv = conv + cst_ref[:, j, :] * cw[j:j + 1, :]
    c = _silu(conv)
    ncst_ref[:, :DN_CONV - 2, :] = cst_ref[:, 1:, :]
    ncst_ref[:, DN_CONV - 2, :] = qkv
    g, beta = _gdn_gates(ab, alog_ref, dtb_ref)

    def emit(hh, qn, kn, vh, g_col, b_col):
        q_ref[:, hh, :] = qn
        k_ref[:, hh, :] = kn
        v_ref[:, hh, :] = vh
        eg_ref[:, hh, :] = jnp.broadcast_to(jnp.exp(g_col), (nb, DN_DK))
        beta_ref[:, hh, :] = jnp.broadcast_to(b_col, (nb, DN_DK))

    _gdn_heads(c, g, beta, emit)


def _gdn_proj_sample(x, cst, gain, win, cw, alog, dtb):
    nb, d = x.shape
    qk = DN_HEADS * DN_DK
    return pl.pallas_call(
        _gdn_proj_sample_body,
        out_shape=[jax.ShapeDtypeStruct((nb, DN_HEADS, DN_DK), F32)] * 5 + [
            jax.ShapeDtypeStruct((nb, qk), F32), jax.ShapeDtypeStruct(cst.shape, F32)],
        compiler_params=_params(0),
        name="gdn_proj_sample",
    )(x, cst, gain, win, cw, alog, dtb)


def _gdn_state_sample_body(q_ref, k_ref, v_ref, eg_ref, beta_ref, s_ref, o_ref, ns_ref, *, nblk):
    hk = DN_HEADS * DN_DK
    row = lax.broadcasted_iota(jnp.int32, (DN_HEADS, hk), 0)
    blk = lax.broadcasted_iota(jnp.int32, (DN_HEADS, hk), 1) // DN_DK
    own = row == blk

    def blockdiag(a):
        return jnp.where(own, jnp.concatenate([a] * DN_HEADS, axis=1), 0.0).astype(BF16)

    for r in range(nblk):
        s_old = s_ref[r]
        kbd = blockdiag(k_ref[r])
        eg = eg_ref[r]
        kts = _dot(kbd, s_old.reshape(hk, DN_DV).astype(BF16))
        w = beta_ref[r] * (v_ref[r] - eg * kts)
        outer = _dot_tn(kbd, w.astype(BF16))
        s_new = s_old * eg[:, None, :] + outer.reshape(DN_HEADS, DN_DK, DN_DV)
        ns_ref[r] = s_new
        o_ref[r] = _dot(blockdiag(q_ref[r]), s_new.reshape(hk, DN_DV).astype(BF16))


def _gdn_state_sample(q, k, v, eg, beta, s, *, nblk):
    nb = s.shape[0]
    body = functools.partial(_gdn_state_sample_body, nblk=nblk)
    vec = pl.BlockSpec((nblk, DN_HEADS, DN_DK), lambda i: (i, 0, 0))
    st = pl.BlockSpec((nblk, DN_HEADS, DN_DK, DN_DV), lambda i: (i, 0, 0, 0))
    return pl.pallas_call(
        body,
        grid=(nb // nblk,),
        in_specs=[vec] * 5 + [st],
        out_specs=[vec, st],
        out_shape=[jax.ShapeDtypeStruct((nb, DN_HEADS, DN_DV), F32),
                   jax.ShapeDtypeStruct(s.shape, F32)],
        compiler_params=_params(1),
        name="gdn_state_sample",
    )(q, k, v, eg, beta, s)


def _gdn_out_sample_body(o_ref, z_ref, x_ref, og_ref, wout_ref, y_ref):
    o_heads = [o_ref[:, hh, :] for hh in range(DN_HEADS)]
    y_ref[...] = _gdn_out(o_heads, z_ref[...], x_ref[...], og_ref, wout_ref)


def _gdn_out_sample(o, z, x, og, wout):
    return pl.pallas_call(
        _gdn_out_sample_body,
        out_shape=jax.ShapeDtypeStruct(x.shape, F32),
        compiler_params=_params(0),
        name="gdn_out_sample",
    )(o, z, x, og, wout)


def _row(a):
    return a.reshape(1, -1).astype(F32)


def _forward(x_prompt, x_sample, state_pool_l0, state_conv_l2, state_delta_l2, state_pool_l3,
             l0_norm, l0_pool_w_in, l0_pool_w_grp, l0_pool_scale, l0_pool_w_out,
             l1_norm, l1_sgu_w_in, l1_sgu_ln_g, l1_sgu_ln_b, l1_sgu_w_s, l1_sgu_b_s, l1_sgu_w_out,
             l2_norm, l2_dn_w_in, l2_dn_conv_w, l2_dn_a_log, l2_dn_dt_bias, l2_dn_o_gain, l2_dn_w_out,
             l3_norm, l3_pool_w_in, l3_pool_w_grp, l3_pool_scale, l3_pool_w_out,
             final_norm, *, tm_pool, tm_sgu, tm_gdn, nblk_state):
    b, t, d = x_prompt.shape
    nb = x_sample.shape[0]
    dp = l0_pool_scale.shape[0]
    ds = l1_sgu_ln_g.shape[0]
    gw_s = ds // SGU_GROUPS
    qk = DN_HEADS * DN_DK
    cch = 3 * qk
    fin = _row(final_norm)
    xs = x_sample

    def pool_layer(xp, xs_, st, norm, w_in, w_grp, scale, w_out, final):
        args = (_row(norm), w_in.astype(BF16), w_grp.astype(BF16), _row(scale), w_out.astype(BF16), fin)
        yp, st_p = _pool_prompt(xp, *args, tm=tm_pool, final_norm=final)
        ys, st_s = _pool_sample(xs_, st, *args, final_norm=final,
                                y_shape=(nb, 1, d) if final else (nb, d))
        return yp, ys, st_p[:, POOL_HIST_PAD - POOL_HIST:, :], st_s

    xp, xs, pool0_p, pool0_s = pool_layer(
        x_prompt, xs, state_pool_l0, l0_norm, l0_pool_w_in, l0_pool_w_grp, l0_pool_scale,
        l0_pool_w_out, False)

    sgu_w_in = l1_sgu_w_in.astype(BF16)
    sgu_w_out = l1_sgu_w_out.astype(BF16)
    bsb = jnp.repeat(l1_sgu_b_s.T.astype(F32), gw_s, axis=1)
    xp = _sgu_prompt(xp.reshape(b * t, d), _row(l1_norm), sgu_w_in, _row(l1_sgu_ln_g),
                     _row(l1_sgu_ln_b), l1_sgu_w_s.astype(F32), bsb, sgu_w_out,
                     tm=tm_sgu).reshape(b, t, d)
    w00 = jnp.repeat(l1_sgu_w_s[:, 0, 0].astype(F32), gw_s)[None, :]
    b0 = jnp.repeat(l1_sgu_b_s[:, 0].astype(F32), gw_s)[None, :]
    xs, sgu1_s = _sgu_sample(xs, _row(l1_norm), sgu_w_in, _row(l1_sgu_ln_g), _row(l1_sgu_ln_b),
                             w00, b0, sgu_w_out)

    n_proj = l2_dn_w_in.shape[1]
    n_pad = cch + qk + LANES
    dn_w_in = jnp.pad(l2_dn_w_in, ((0, 0), (0, n_pad - n_proj))).astype(BF16)
    dn_w_out = l2_dn_w_out.astype(BF16)
    alog = jnp.pad(l2_dn_a_log.astype(F32), (0, LANES - DN_HEADS))[None, :]
    dtb = jnp.pad(l2_dn_dt_bias.astype(F32), (0, LANES - DN_HEADS))[None, :]
    cw = l2_dn_conv_w.astype(F32)
    og = _row(l2_dn_o_gain)
    q, k, kb, vb, gam, z, conv2_p = _gdn_proj_prompt(xp, _row(l2_norm), dn_w_in, cw, alog, dtb,
                                                     tm=tm_gdn)
    xp, dn2_p = _gdn_delta_prompt(q, k, kb, vb, gam, z, xp, og, dn_w_out, tm=tm_gdn)
    conv2_p = conv2_p[:, CONV_HIST_PAD - (DN_CONV - 1):, :]

    qs, ks, vs, egs, betas, zs, conv2_s = _gdn_proj_sample(
        xs, state_conv_l2, _row(l2_norm), dn_w_in, cw, alog, dtb)
    o_s, dn2_s = _gdn_state_sample(qs, ks, vs, egs, betas, state_delta_l2, nblk=nblk_state)
    xs = _gdn_out_sample(o_s, zs, xs, og, dn_w_out)

    yp, ys, pool3_p, pool3_s = pool_layer(
        xp, xs, state_pool_l3, l3_norm, l3_pool_w_in, l3_pool_w_grp, l3_pool_scale,
        l3_pool_w_out, True)

    return (yp, ys, pool0_p, pool0_s, sgu1_s,
            conv2_p, conv2_s, dn2_p, dn2_s, pool3_p, pool3_s)


def kernel(x_prompt, x_sample, state_pool_l0, state_conv_l2, state_delta_l2, state_pool_l3, l0_norm, l0_pool_w_in, l0_pool_w_grp, l0_pool_scale, l0_pool_w_out, l1_norm, l1_sgu_w_in, l1_sgu_ln_g, l1_sgu_ln_b, l1_sgu_w_s, l1_sgu_b_s, l1_sgu_w_out, l2_norm, l2_dn_w_in, l2_dn_conv_w, l2_dn_a_log, l2_dn_dt_bias, l2_dn_o_gain, l2_dn_w_out, l3_norm, l3_pool_w_in, l3_pool_w_grp, l3_pool_scale, l3_pool_w_out, final_norm):
    return _forward(
        x_prompt, x_sample, state_pool_l0, state_conv_l2, state_delta_l2, state_pool_l3,
        l0_norm, l0_pool_w_in, l0_pool_w_grp, l0_pool_scale, l0_pool_w_out,
        l1_norm, l1_sgu_w_in, l1_sgu_ln_g, l1_sgu_ln_b, l1_sgu_w_s, l1_sgu_b_s, l1_sgu_w_out,
        l2_norm, l2_dn_w_in, l2_dn_conv_w, l2_dn_a_log, l2_dn_dt_bias, l2_dn_o_gain, l2_dn_w_out,
        l3_norm, l3_pool_w_in, l3_pool_w_grp, l3_pool_scale, l3_pool_w_out,
        final_norm, tm_pool=512, tm_sgu=256, tm_gdn=256, nblk_state=8)
```

```python
import functools

import jax
import jax.numpy as jnp
from jax import lax
from jax.experimental import pallas as pl
from jax.experimental.pallas import tpu as pltpu

F32 = jnp.float32
BF16 = jnp.bfloat16
EPS = 1e-6
PAST_LEN = 16384
POOL_WINDOWS = (2, 4, 8, 16)
POOL_HIST = 15
POOL_HIST_PAD = 16
SGU_CHUNK = 128
SGU_GROUPS = 4
DN_HEADS = 8
DN_DK = 128
DN_DV = 128
DN_CONV = 4
CONV_HIST_PAD = 8
DN_CHUNK = 128
INV_BASE_BITS = 4
LANES = 128
VMEM_LIMIT = 60 * 1024 * 1024


def _rmsnorm(x, gain):
    ms = jnp.mean(x * x, axis=-1, keepdims=True)
    return x * lax.rsqrt(ms + EPS) * gain


def _silu(z):
    return z * jax.nn.sigmoid(z)


def _gelu(x):
    return 0.5 * x * (1.0 + lax.erf(x * (2.0 ** -0.5)))


def _dot(a, b):
    return jnp.dot(a, b, preferred_element_type=F32)


def _dot_nt(a, b):
    return lax.dot_general(a, b, (((1,), (1,)), ((), ())), preferred_element_type=F32)


def _dot_tn(a, b):
    return lax.dot_general(a, b, (((0,), (0,)), ((), ())), preferred_element_type=F32)


def _load_rows(ref):
    return ref[:, 0, :] if len(ref.shape) == 3 else ref[...]


def _store_rows(ref, val):
    if len(ref.shape) == 3:
        ref[:, 0, :] = val
    else:
        ref[...] = val


def _const_spec(shape):
    nd = len(shape)
    return pl.BlockSpec(shape, lambda *_: (0,) * nd, pipeline_mode=pl.Buffered(1))


def _params(n_grid):
    return pltpu.CompilerParams(
        dimension_semantics=("arbitrary",) * n_grid, vmem_limit_bytes=VMEM_LIMIT)


def _pool_tail(pooled_groups, z, x, wgrp_ref, scale_ref, wout_ref, fin_ref, final_norm):
    mixed = [_dot(p.astype(BF16), wgrp_ref[gi]) for gi, p in enumerate(pooled_groups)]
    mixed = jnp.concatenate(mixed, axis=1)
    gated = mixed * scale_ref[...] * _silu(z)
    y = _dot(gated.astype(BF16), wout_ref[...]) + x
    if final_norm:
        y = _rmsnorm(y, fin_ref[...])
    return y


def _pool_prompt_body(x_ref, gain_ref, win_ref, wgrp_ref, scale_ref, wout_ref, fin_ref,
                      y_ref, st_ref, hist_ref, *, tm, n_t, final_norm):
    t = pl.program_id(1)
    dp = scale_ref.shape[-1]
    gw = dp // len(POOL_WINDOWS)

    @pl.when(t == 0)
    def _():
        hist_ref[...] = jnp.zeros_like(hist_ref)

    x = x_ref[...]
    h = _rmsnorm(x, gain_ref[...]).astype(BF16)
    xz = _dot(h, win_ref[...])
    xb = xz[:, :dp]
    z = xz[:, dp:]
    ext = jnp.concatenate([hist_ref[...], xb], axis=0)
    last = xb[tm - POOL_HIST_PAD:, :]
    hist_ref[...] = last
    pos1 = lax.broadcasted_iota(jnp.int32, (tm, LANES), 0) + (t * tm + 1)
    pooled = []
    for gi, w in enumerate(POOL_WINDOWS):
        e = ext[:, gi * gw:(gi + 1) * gw]
        s = e + pltpu.roll(e, 1, 0)
        sh = 2
        while sh < w:
            s = s + pltpu.roll(s, sh, 0)
            sh *= 2
        s = s[POOL_HIST_PAD:, :]
        inv = 1.0 / jnp.minimum(pos1, w).astype(F32)
        inv = jnp.concatenate([inv] * (gw // LANES), axis=1)
        pooled.append(s * inv - xb[:, gi * gw:(gi + 1) * gw])
    y_ref[...] = _pool_tail(pooled, z, x, wgrp_ref, scale_ref, wout_ref, fin_ref, final_norm)

    @pl.when(t == n_t - 1)
    def _():
        st_ref[...] = last


def _pool_prompt(x, gain, win, wgrp, scale, wout, fin, *, tm, final_norm):
    b, t, d = x.shape
    dp = scale.shape[-1]
    n_t = t // tm
    body = functools.partial(_pool_prompt_body, tm=tm, n_t=n_t, final_norm=final_norm)
    return pl.pallas_call(
        body,
        grid=(b, n_t),
        in_specs=[
            pl.BlockSpec((None, tm, d), lambda i, j: (i, j, 0)),
            _const_spec(gain.shape), _const_spec(win.shape), _const_spec(wgrp.shape),
            _const_spec(scale.shape), _const_spec(wout.shape), _const_spec(fin.shape),
        ],
        out_specs=[
            pl.BlockSpec((None, tm, d), lambda i, j: (i, j, 0)),
            pl.BlockSpec((None, POOL_HIST_PAD, dp), lambda i, j: (i, 0, 0)),
        ],
        out_shape=[jax.ShapeDtypeStruct((b, t, d), F32),
                   jax.ShapeDtypeStruct((b, POOL_HIST_PAD, dp), F32)],
        scratch_shapes=[pltpu.VMEM((POOL_HIST_PAD, dp), F32)],
        compiler_params=_params(2),
        name="pool_prompt",
    )(x, gain, win, wgrp, scale, wout, fin)


def _pool_sample_body(x_ref, st_ref, gain_ref, win_ref, wgrp_ref, scale_ref, wout_ref, fin_ref,
                      y_ref, nst_ref, *, final_norm):
    dp = scale_ref.shape[-1]
    gw = dp // len(POOL_WINDOWS)
    x = _load_rows(x_ref)
    h = _rmsnorm(x, gain_ref[...]).astype(BF16)
    xz = _dot(h, win_ref[...])
    xb = xz[:, :dp]
    z = xz[:, dp:]
    pooled = []
    for gi, w in enumerate(POOL_WINDOWS):
        xg = xb[:, gi * gw:(gi + 1) * gw]
        s = xg
        for k in range(1, w):
            s = s + st_ref[POOL_HIST - k, :, gi * gw:(gi + 1) * gw]
        cnt = float(min(PAST_LEN + 1, w))
        pooled.append(s * (1.0 / cnt) - xg)
    _store_rows(y_ref, _pool_tail(pooled, z, x, wgrp_ref, scale_ref, wout_ref, fin_ref, final_norm))
    nst_ref[:POOL_HIST - 1] = st_ref[1:]
    nst_ref[POOL_HIST - 1] = xb


def _pool_sample(x, st, gain, win, wgrp, scale, wout, fin, *, final_norm, y_shape):
    body = functools.partial(_pool_sample_body, final_norm=final_norm)
    return pl.pallas_call(
        body,
        out_shape=[jax.ShapeDtypeStruct(y_shape, F32), jax.ShapeDtypeStruct(st.shape, F32)],
        compiler_params=_params(0),
        name="pool_sample",
    )(x, st, gain, win, wgrp, scale, wout, fin)


def _sgu_front(x, gain_ref, win_ref, lng_ref, lnb_ref):
    ds = lng_ref.shape[-1]
    h = _rmsnorm(x, gain_ref[...]).astype(BF16)
    uvz = _dot(h, win_ref[...])
    u = _gelu(uvz[:, :ds])
    v = _gelu(uvz[:, ds:2 * ds])
    z = uvz[:, 2 * ds:]
    mu = jnp.mean(v, axis=-1, keepdims=True)
    vc = v - mu
    var = jnp.mean(vc * vc, axis=-1, keepdims=True)
    vn = vc * lax.rsqrt(var + EPS) * lng_ref[...] + lnb_ref[...]
    return u, vn, z


def _sgu_prompt_body(x_ref, gain_ref, win_ref, lng_ref, lnb_ref, ws_ref, bsb_ref, wout_ref,
                     y_ref, *, tm):
    ds = lng_ref.shape[-1]
    gw = ds // SGU_GROUPS
    x = x_ref[...]
    u, vn, z = _sgu_front(x, gain_ref, win_ref, lng_ref, lnb_ref)
    ri = lax.broadcasted_iota(jnp.int32, (SGU_CHUNK, SGU_CHUNK), 0)
    ci = lax.broadcasted_iota(jnp.int32, (SGU_CHUNK, SGU_CHUNK), 1)
    ws = [jnp.where(ri >= ci, ws_ref[g], 0.0).astype(BF16) for g in range(SGU_GROUPS)]
    vnb = vn.astype(BF16)
    bsb = bsb_ref[...]
    rows = []
    for n in range(tm // SGU_CHUNK):
        r0 = n * SGU_CHUNK
        cols = [_dot(ws[g], vnb[r0:r0 + SGU_CHUNK, g * gw:(g + 1) * gw]) for g in range(SGU_GROUPS)]
        rows.append(jnp.concatenate(cols, axis=1) + bsb)
    s = jnp.concatenate(rows, axis=0)
    gated = u * s * _silu(z)
    y_ref[...] = _dot(gated.astype(BF16), wout_ref[...]) + x


def _sgu_prompt(x, gain, win, lng, lnb, ws, bsb, wout, *, tm):
    n, d = x.shape
    body = functools.partial(_sgu_prompt_body, tm=tm)
    return pl.pallas_call(
        body,
        grid=(n // tm,),
        in_specs=[pl.BlockSpec((tm, d), lambda i: (i, 0))] + [
            _const_spec(a.shape) for a in (gain, win, lng, lnb, ws, bsb, wout)],
        out_specs=pl.BlockSpec((tm, d), lambda i: (i, 0)),
        out_shape=jax.ShapeDtypeStruct((n, d), F32),
        compiler_params=_params(1),
        name="sgu_prompt",
    )(x, gain, win, lng, lnb, ws, bsb, wout)


def _sgu_sample_body(x_ref, gain_ref, win_ref, lng_ref, lnb_ref, w00_ref, b0_ref, wout_ref,
                     y_ref, v_ref):
    x = x_ref[...]
    u, vn, z = _sgu_front(x, gain_ref, win_ref, lng_ref, lnb_ref)
    s = vn * w00_ref[...] + b0_ref[...]
    gated = u * s * _silu(z)
    y_ref[...] = _dot(gated.astype(BF16), wout_ref[...]) + x
    _store_rows(v_ref, vn)


def _sgu_sample(x, gain, win, lng, lnb, w00, b0, wout):
    nb, d = x.shape
    ds = lng.shape[-1]
    return pl.pallas_call(
        _sgu_sample_body,
        out_shape=[jax.ShapeDtypeStruct((nb, d), F32), jax.ShapeDtypeStruct((nb, 1, ds), F32)],
        compiler_params=_params(0),
        name="sgu_sample",
    )(x, gain, win, lng, lnb, w00, b0, wout)


def _gdn_gates(ab, alog_ref, dtb_ref):
    g = -jnp.exp(alog_ref[...]) * jax.nn.softplus(ab + dtb_ref[...])
    beta = jax.nn.sigmoid(ab)
    return g, beta


def _lane_pick(a, lane, idx):
    return jnp.sum(jnp.where(lane == idx, a, 0.0), axis=-1, keepdims=True)


def _gdn_heads(c, g_like, beta_all, emit):
    rows = c.shape[0]
    qk = DN_HEADS * DN_DK
    lane = lax.broadcasted_iota(jnp.int32, (rows, LANES), 1)
    for hh in range(DN_HEADS):
        qh = c[:, hh * DN_DK:(hh + 1) * DN_DK]
        kh = c[:, qk + hh * DN_DK:qk + (hh + 1) * DN_DK]
        vh = c[:, 2 * qk + hh * DN_DV:2 * qk + (hh + 1) * DN_DV]
        qn = qh * lax.rsqrt(jnp.sum(qh * qh, axis=-1, keepdims=True) + EPS) * (DN_DK ** -0.5)
        kn = kh * lax.rsqrt(jnp.sum(kh * kh, axis=-1, keepdims=True) + EPS)
        emit(hh, qn, kn, vh, _lane_pick(g_like, lane, hh), _lane_pick(beta_all, lane, DN_HEADS + hh))


def _gdn_proj_prompt_body(x_ref, gain_ref, win_ref, cw_ref, alog_ref, dtb_ref,
                          q_ref, k_ref, kb_ref, vb_ref, gam_ref, z_ref, cst_ref, hist_ref,
                          *, tm, n_t):
    t = pl.program_id(1)
    qk = DN_HEADS * DN_DK
    cch = 3 * qk

    @pl.when(t == 0)
    def _():
        hist_ref[...] = jnp.zeros_like(hist_ref)

    x = x_ref[...]
    h = _rmsnorm(x, gain_ref[...]).astype(BF16)
    proj = _dot(h, win_ref[...])
    qkv = proj[:, :cch]
    z_ref[...] = proj[:, cch:cch + qk]
    ab = proj[:, cch + qk:cch + qk + LANES]
    ext = jnp.concatenate([hist_ref[...], qkv], axis=0)
    last = qkv[tm - CONV_HIST_PAD:, :]
    hist_ref[...] = last
    cw = cw_ref[...]
    conv = ext * cw[DN_CONV - 1:DN_CONV, :]
    for j in range(1, DN_CONV):
        conv = conv + pltpu.roll(ext, j, 0) * cw[DN_CONV - 1 - j:DN_CONV - j, :]
    c = _silu(conv[CONV_HIST_PAD:, :])

    g, beta = _gdn_gates(ab, alog_ref, dtb_ref)
    rin = lax.broadcasted_iota(jnp.int32, (tm, LANES), 0) & (DN_CHUNK - 1)
    gam = g
    sh = 1
    while sh < DN_CHUNK:
        gam = gam + jnp.where(rin >= sh, pltpu.roll(gam, sh, 0), 0.0)
        sh *= 2

    def emit(hh, qn, kn, vh, g_col, b_col):
        sl = slice(hh * DN_DK, (hh + 1) * DN_DK)
        q_ref[:, sl] = qn.astype(BF16)
        k_ref[:, sl] = kn.astype(BF16)
        kb_ref[:, sl] = (b_col * kn).astype(BF16)
        vb_ref[:, sl] = (b_col * vh).astype(BF16)
        gam_ref[:, sl] = jnp.broadcast_to(g_col, (tm, DN_DK))

    _gdn_heads(c, gam, beta, emit)

    @pl.when(t == n_t - 1)
    def _():
        cst_ref[...] = last


def _gdn_proj_prompt(x, gain, win, cw, alog, dtb, *, tm):
    b, t, d = x.shape
    qk = DN_HEADS * DN_DK
    cch = 3 * qk
    n_t = t // tm
    body = functools.partial(_gdn_proj_prompt_body, tm=tm, n_t=n_t)
    tok = lambda i, j: (i, j, 0)
    return pl.pallas_call(
        body,
        grid=(b, n_t),
        in_specs=[pl.BlockSpec((None, tm, d), tok)] + [
            _const_spec(a.shape) for a in (gain, win, cw, alog, dtb)],
        out_specs=[pl.BlockSpec((None, tm, qk), tok)] * 6 + [
            pl.BlockSpec((None, CONV_HIST_PAD, cch), lambda i, j: (i, 0, 0))],
        out_shape=[jax.ShapeDtypeStruct((b, t, qk), BF16)] * 4 + [
            jax.ShapeDtypeStruct((b, t, qk), F32)] * 2 + [
            jax.ShapeDtypeStruct((b, CONV_HIST_PAD, cch), F32)],
        scratch_shapes=[pltpu.VMEM((CONV_HIST_PAD, cch), F32)],
        compiler_params=_params(2),
        name="gdn_proj_prompt",
    )(x, gain, win, cw, alog, dtb)


def _unit_lower_inverse_minus_eye(ms, in_base, at_level):
    bf = lambda xs: [x.astype(BF16) for x in xs]
    a = [jnp.where(in_base, m, 0.0) for m in ms]
    e = [-x for x in a]
    ab = bf(a)
    p = [_dot(x, x) for x in ab]
    for it in range(INV_BASE_BITS - 1):
        pb = bf(p)
        ep = [_dot(x, y) for x, y in zip(bf(e), pb)]
        e = [x + y + z for x, y, z in zip(e, p, ep)]
        if it < INV_BASE_BITS - 2:
            p = [_dot(x, x) for x in pb]
    for mask in at_level:
        low = [jnp.where(mask, m, 0.0) for m in ms]
        eb = bf(e)
        x = [lo + _dot(lb, y) for lo, lb, y in zip(low, bf(low), eb)]
        ex = [_dot(y, xb) for y, xb in zip(eb, bf(x))]
        e = [ei - (xi + yi) for ei, xi, yi in zip(e, x, ex)]
    return e


def _gdn_out(o_heads, z, x, og_ref, wout_ref):
    normed = [oh * lax.rsqrt(jnp.mean(oh * oh, axis=-1, keepdims=True) + EPS) * og_ref[...]
              for oh in o_heads]
    gated = jnp.concatenate(normed, axis=1) * _silu(z)
    return _dot(gated.astype(BF16), wout_ref[...]) + x


def _gdn_delta_prompt_body(q_ref, k_ref, kb_ref, vb_ref, gam_ref, z_ref, x_ref, og_ref, wout_ref,
                           y_ref, sfin_ref, s_ref, o_ref, *, tm, n_t):
    t = pl.program_id(1)

    @pl.when(t == 0)
    def _():
        s_ref[...] = jnp.zeros_like(s_ref)

    c = DN_CHUNK
    ri = lax.broadcasted_iota(jnp.int32, (c, c), 0)
    ci = lax.broadcasted_iota(jnp.int32, (c, c), 1)
    incl = ri >= ci
    strict = ri > ci
    top = ri ^ ci
    in_base = (top >> INV_BASE_BITS) == 0
    levels = range(INV_BASE_BITS, c.bit_length() - 1)
    at_level = [(top >> lv) == 1 for lv in levels]
    n_chunks = tm // c
    pairs = [(n, hh) for n in range(n_chunks) for hh in range(DN_HEADS)]

    def blk(ref, n, hh):
        return ref[n * c:(n + 1) * c, hh * DN_DK:(hh + 1) * DN_DK]

    kh = [blk(k_ref, *p) for p in pairs]
    kbh = [blk(kb_ref, *p) for p in pairs]
    gam = [blk(gam_ref, *p) for p in pairs]
    kk = [_dot_nt(a, b) for a, b in zip(kbh, kh)]
    dec = [jnp.exp(jnp.where(incl, g - g.T, -1e30)) for g in gam]
    m = [jnp.where(strict, a * d, 0.0) for a, d in zip(kk, dec)]
    e = _unit_lower_inverse_minus_eye(m, in_base, at_level)
    eg = [jnp.exp(g) for g in gam]
    rhs = [jnp.concatenate([blk(vb_ref, *p).astype(F32), a.astype(F32) * x], axis=1)
           for p, a, x in zip(pairs, kbh, eg)]
    uw = [r + _dot(x.astype(BF16), r.astype(BF16)) for r, x in zip(rhs, e)]
    qh = [blk(q_ref, *p) for p in pairs]
    qkm = [(_dot_nt(a, b) * d).astype(BF16) for a, b, d in zip(qh, kh, dec)]
    qd = [(a.astype(F32) * x).astype(BF16) for a, x in zip(qh, eg)]
    gl = [g[c - 1:c, :] for g in gam]
    kt = [(a.astype(F32) * jnp.exp(l - g)).astype(BF16) for a, l, g in zip(kh, gl, gam)]
    al = [jnp.exp(l) for l in gl]

    for n in range(n_chunks):
        idx = [n * DN_HEADS + hh for hh in range(DN_HEADS)]
        s_old = [s_ref[hh] for hh in range(DN_HEADS)]
        sb = [x.astype(BF16) for x in s_old]
        w = [uw[i][:, :DN_DV] - _dot(uw[i][:, DN_DV:].astype(BF16), y) for i, y in zip(idx, sb)]
        wb = [x.astype(BF16) for x in w]
        for hh, i in enumerate(idx):
            o_ref[n * c:(n + 1) * c, hh * DN_DV:(hh + 1) * DN_DV] = (
                _dot(qd[i], sb[hh]) + _dot(qkm[i], wb[hh]))
        for hh, i in enumerate(idx):
            s_ref[hh] = al[i] * s_old[hh] + _dot_tn(kt[i], wb[hh])
    o = o_ref[...]
    o_heads = [o[:, hh * DN_DV:(hh + 1) * DN_DV] for hh in range(DN_HEADS)]
    y_ref[...] = _gdn_out(o_heads, z_ref[...], x_ref[...], og_ref, wout_ref)

    @pl.when(t == n_t - 1)
    def _():
        sfin_ref[...] = s_ref[...]


def _gdn_delta_prompt(q, k, kb, vb, gam, z, x, og, wout, *, tm):
    b, t, d = x.shape
    qk = DN_HEADS * DN_DK
    n_t = t // tm
    body = functools.partial(_gdn_delta_prompt_body, tm=tm, n_t=n_t)
    tok = lambda i, j: (i, j, 0)
    return pl.pallas_call(
        body,
        grid=(b, n_t),
        in_specs=[pl.BlockSpec((None, tm, qk), tok)] * 6 + [pl.BlockSpec((None, tm, d), tok),
                  _const_spec(og.shape), _const_spec(wout.shape)],
        out_specs=[pl.BlockSpec((None, tm, d), tok),
                   pl.BlockSpec((None, DN_HEADS, DN_DK, DN_DV), lambda i, j: (i, 0, 0, 0))],
        out_shape=[jax.ShapeDtypeStruct((b, t, d), F32),
                   jax.ShapeDtypeStruct((b, DN_HEADS, DN_DK, DN_DV), F32)],
        scratch_shapes=[pltpu.VMEM((DN_HEADS, DN_DK, DN_DV), F32),
                        pltpu.VMEM((tm, DN_HEADS * DN_DV), F32)],
        compiler_params=_params(2),
        name="gdn_delta_prompt",
    )(q, k, kb, vb, gam, z, x, og, wout)


def _gdn_proj_sample_body(x_ref, cst_ref, gain_ref, win_ref, cw_ref, alog_ref, dtb_ref,
                          q_ref, k_ref, v_ref, eg_ref, beta_ref, z_ref, ncst_ref):
    qk = DN_HEADS * DN_DK
    cch = 3 * qk
    nb = x_ref.shape[0]
    x = x_ref[...]
    h = _rmsnorm(x, gain_ref[...]).astype(BF16)
    proj = _dot(h, win_ref[...])
    qkv = proj[:, :cch]
    z_ref[...] = proj[:, cch:cch + qk]
    ab = proj[:, cch + qk:cch + qk + LANES]
    cw = cw_ref[...]
    conv = qkv * cw[DN_CONV - 1:DN_CONV, :]
    for j in range(DN_CONV - 1):
        conv = conv + cst_ref[j] * cw[j:j + 1, :]
    c = _silu(conv)
    ncst_ref[:DN_CONV - 2] = cst_ref[1:]
    ncst_ref[DN_CONV - 2] = qkv
    g, beta = _gdn_gates(ab, alog_ref, dtb_ref)

    def emit(hh, qn, kn, vh, g_col, b_col):
        q_ref[:, hh, :] = qn
        k_ref[:, hh, :] = kn
        v_ref[:, hh, :] = vh
        eg_ref[:, hh, :] = jnp.broadcast_to(jnp.exp(g_col), (nb, DN_DK))
        beta_ref[:, hh, :] = jnp.broadcast_to(b_col, (nb, DN_DK))

    _gdn_heads(c, g, beta, emit)


def _gdn_proj_sample(x, cst, gain, win, cw, alog, dtb):
    nb, d = x.shape
    qk = DN_HEADS * DN_DK
    return pl.pallas_call(
        _gdn_proj_sample_body,
        out_shape=[jax.ShapeDtypeStruct((nb, DN_HEADS, DN_DK), F32)] * 5 + [
            jax.ShapeDtypeStruct((nb, qk), F32), jax.ShapeDtypeStruct(cst.shape, F32)],
        compiler_params=_params(0),
        name="gdn_proj_sample",
    )(x, cst, gain, win, cw, alog, dtb)


def _gdn_state_sample_body(q_ref, k_ref, v_ref, eg_ref, beta_ref, s_ref, o_ref, ns_ref, *, nblk):
    hk = DN_HEADS * DN_DK
    row = lax.broadcasted_iota(jnp.int32, (DN_HEADS, hk), 0)
    blk = lax.broadcasted_iota(jnp.int32, (DN_HEADS, hk), 1) // DN_DK
    own = row == blk

    def blockdiag(a):
        return jnp.where(own, jnp.concatenate([a] * DN_HEADS, axis=1), 0.0).astype(BF16)

    for r in range(nblk):
        s_old = s_ref[r]
        kbd = blockdiag(k_ref[r])
        eg = eg_ref[r]
        kts = _dot(kbd, s_old.reshape(hk, DN_DV).astype(BF16))
        w = beta_ref[r] * (v_ref[r] - eg * kts)
        outer = _dot_tn(kbd, w.astype(BF16))
        s_new = s_old * eg[:, None, :] + outer.reshape(DN_HEADS, DN_DK, DN_DV)
        ns_ref[r] = s_new
        o_ref[r] = _dot(blockdiag(q_ref[r]), s_new.reshape(hk, DN_DV).astype(BF16))


def _gdn_state_sample(q, k, v, eg, beta, s, *, nblk):
    nb = s.shape[0]
    body = functools.partial(_gdn_state_sample_body, nblk=nblk)
    vec = pl.BlockSpec((nblk, DN_HEADS, DN_DK), lambda i: (i, 0, 0))
    st = pl.BlockSpec((nblk, DN_HEADS, DN_DK, DN_DV), lambda i: (i, 0, 0, 0))
    return pl.pallas_call(
        body,
        grid=(nb // nblk,),
        in_specs=[vec] * 5 + [st],
        out_specs=[vec, st],
        out_shape=[jax.ShapeDtypeStruct((nb, DN_HEADS, DN_DV), F32),
                   jax.ShapeDtypeStruct(s.shape, F32)],
        compiler_params=_params(1),
        name="gdn_state_sample",
    )(q, k, v, eg, beta, s)


def _gdn_out_sample_body(o_ref, z_ref, x_ref, og_ref, wout_ref, y_ref):
    o_heads = [o_ref[:, hh, :] for hh in range(DN_HEADS)]
    y_ref[...] = _gdn_out(o_heads, z_ref[...], x_ref[...], og_ref, wout_ref)


def _gdn_out_sample(o, z, x, og, wout):
    return pl.pallas_call(
        _gdn_out_sample_body,
        out_shape=jax.ShapeDtypeStruct(x.shape, F32),
        compiler_params=_params(0),
        name="gdn_out_sample",
    )(o, z, x, og, wout)


def _row(a):
    return a.reshape(1, -1).astype(F32)


def _time_major(st):
    return jnp.transpose(st, (1, 0, 2))


def _forward(x_prompt, x_sample, state_pool_l0, state_conv_l2, state_delta_l2, state_pool_l3,
             l0_norm, l0_pool_w_in, l0_pool_w_grp, l0_pool_scale, l0_pool_w_out,
             l1_norm, l1_sgu_w_in, l1_sgu_ln_g, l1_sgu_ln_b, l1_sgu_w_s, l1_sgu_b_s, l1_sgu_w_out,
             l2_norm, l2_dn_w_in, l2_dn_conv_w, l2_dn_a_log, l2_dn_dt_bias, l2_dn_o_gain, l2_dn_w_out,
             l3_norm, l3_pool_w_in, l3_pool_w_grp, l3_pool_scale, l3_pool_w_out,
             final_norm, *, tm_pool, tm_sgu, tm_gdn_proj, tm_gdn, nblk_state):
    b, t, d = x_prompt.shape
    nb = x_sample.shape[0]
    dp = l0_pool_scale.shape[0]
    ds = l1_sgu_ln_g.shape[0]
    gw_s = ds // SGU_GROUPS
    qk = DN_HEADS * DN_DK
    cch = 3 * qk
    fin = _row(final_norm)
    xs = x_sample

    def pool_layer(xp, xs_, st, norm, w_in, w_grp, scale, w_out, final):
        args = (_row(norm), w_in.astype(BF16), w_grp.astype(BF16), _row(scale), w_out.astype(BF16), fin)
        yp, st_p = _pool_prompt(xp, *args, tm=tm_pool, final_norm=final)
        ys, st_s = _pool_sample(xs_, _time_major(st), *args, final_norm=final,
                                y_shape=(nb, 1, d) if final else (nb, d))
        return yp, ys, st_p[:, POOL_HIST_PAD - POOL_HIST:, :], _time_major(st_s)

    xp, xs, pool0_p, pool0_s = pool_layer(
        x_prompt, xs, state_pool_l0, l0_norm, l0_pool_w_in, l0_pool_w_grp, l0_pool_scale,
        l0_pool_w_out, False)

    sgu_w_in = l1_sgu_w_in.astype(BF16)
    sgu_w_out = l1_sgu_w_out.astype(BF16)
    bsb = jnp.repeat(l1_sgu_b_s.T.astype(F32), gw_s, axis=1)
    xp = _sgu_prompt(xp.reshape(b * t, d), _row(l1_norm), sgu_w_in, _row(l1_sgu_ln_g),
                     _row(l1_sgu_ln_b), l1_sgu_w_s.astype(F32), bsb, sgu_w_out,
                     tm=tm_sgu).reshape(b, t, d)
    w00 = jnp.repeat(l1_sgu_w_s[:, 0, 0].astype(F32), gw_s)[None, :]
    b0 = jnp.repeat(l1_sgu_b_s[:, 0].astype(F32), gw_s)[None, :]
    xs, sgu1_s = _sgu_sample(xs, _row(l1_norm), sgu_w_in, _row(l1_sgu_ln_g), _row(l1_sgu_ln_b),
                             w00, b0, sgu_w_out)

    n_proj = l2_dn_w_in.shape[1]
    n_pad = cch + qk + LANES
    dn_w_in = jnp.pad(l2_dn_w_in, ((0, 0), (0, n_pad - n_proj))).astype(BF16)
    dn_w_out = l2_dn_w_out.astype(BF16)
    alog = jnp.pad(l2_dn_a_log.astype(F32), (0, LANES - DN_HEADS))[None, :]
    dtb = jnp.pad(l2_dn_dt_bias.astype(F32), (0, LANES - DN_HEADS))[None, :]
    cw = l2_dn_conv_w.astype(F32)
    og = _row(l2_dn_o_gain)
    q, k, kb, vb, gam, z, conv2_p = _gdn_proj_prompt(xp, _row(l2_norm), dn_w_in, cw, alog, dtb,
                                                     tm=tm_gdn_proj)
    xp, dn2_p = _gdn_delta_prompt(q, k, kb, vb, gam, z, xp, og, dn_w_out, tm=tm_gdn)
    conv2_p = conv2_p[:, CONV_HIST_PAD - (DN_CONV - 1):, :]

    qs, ks, vs, egs, betas, zs, conv2_s = _gdn_proj_sample(
        xs, _time_major(state_conv_l2), _row(l2_norm), dn_w_in, cw, alog, dtb)
    conv2_s = _time_major(conv2_s)
    o_s, dn2_s = _gdn_state_sample(qs, ks, vs, egs, betas, state_delta_l2, nblk=nblk_state)
    xs = _gdn_out_sample(o_s, zs, xs, og, dn_w_out)

    yp, ys, pool3_p, pool3_s = pool_layer(
        xp, xs, state_pool_l3, l3_norm, l3_pool_w_in, l3_pool_w_grp, l3_pool_scale,
        l3_pool_w_out, True)

    return (yp, ys, pool0_p, pool0_s, sgu1_s,
            conv2_p, conv2_s, dn2_p, dn2_s, pool3_p, pool3_s)


def kernel(x_prompt, x_sample, state_pool_l0, state_conv_l2, state_delta_l2, state_pool_l3, l0_norm, l0_pool_w_in, l0_pool_w_grp, l0_pool_scale, l0_pool_w_out, l1_norm, l1_sgu_w_in, l1_sgu_ln_g, l1_sgu_ln_b, l1_sgu_w_s, l1_sgu_b_s, l1_sgu_w_out, l2_norm, l2_dn_w_in, l2_dn_conv_w, l2_dn_a_log, l2_dn_dt_bias, l2_dn_o_gain, l2_dn_w_out, l3_norm, l3_pool_w_in, l3_pool_w_grp, l3_pool_scale, l3_pool_w_out, final_norm):
    return _forward(
        x_prompt, x_sample, state_pool_l0, state_conv_l2, state_delta_l2, state_pool_l3,
        l0_norm, l0_pool_w_in, l0_pool_w_grp, l0_pool_scale, l0_pool_w_out,
        l1_norm, l1_sgu_w_in, l1_sgu_ln_g, l1_sgu_ln_b, l1_sgu_w_s, l1_sgu_b_s, l1_sgu_w_out,
        l2_norm, l2_dn_w_in, l2_dn_conv_w, l2_dn_a_log, l2_dn_dt_bias, l2_dn_o_gain, l2_dn_w_out,
        l3_norm, l3_pool_w_in, l3_pool_w_grp, l3_pool_scale, l3_pool_w_out,
        final_norm, tm_pool=512, tm_sgu=512, tm_gdn_proj=512, tm_gdn=256, nblk_state=8)
```

```python
import functools

import jax
import jax.numpy as jnp
from jax import lax
from jax.experimental import pallas as pl
from jax.experimental.pallas import tpu as pltpu

F32 = jnp.float32
BF16 = jnp.bfloat16
EPS = 1e-6
PAST_LEN = 16384
POOL_WINDOWS = (2, 4, 8, 16)
POOL_HIST = 15
POOL_HIST_PAD = 16
SGU_CHUNK = 128
SGU_GROUPS = 4
DN_HEADS = 8
DN_DK = 128
DN_DV = 128
DN_CONV = 4
CONV_HIST_PAD = 8
DN_CHUNK = 128
INV_BASE_BITS = 4
LANES = 128
N_PREP = 8
VMEM_LIMIT = 60 * 1024 * 1024


def _rmsnorm(x, gain):
    ms = jnp.mean(x * x, axis=-1, keepdims=True)
    return x * lax.rsqrt(ms + EPS) * gain


def _silu(z):
    return z * jax.nn.sigmoid(z)


def _gelu(x):
    return 0.5 * x * (1.0 + lax.erf(x * (2.0 ** -0.5)))


def _dot(a, b):
    return jnp.dot(a, b, preferred_element_type=F32)


def _dot_nt(a, b):
    return lax.dot_general(a, b, (((1,), (1,)), ((), ())), preferred_element_type=F32)


def _dot_tn(a, b):
    return lax.dot_general(a, b, (((0,), (0,)), ((), ())), preferred_element_type=F32)


def _load_rows(ref):
    return ref[:, 0, :] if len(ref.shape) == 3 else ref[...]


def _store_rows(ref, val):
    if len(ref.shape) == 3:
        ref[:, 0, :] = val
    else:
        ref[...] = val


def _const_spec(shape):
    nd = len(shape)
    return pl.BlockSpec(shape, lambda *_: (0,) * nd, pipeline_mode=pl.Buffered(1))


def _params(n_grid):
    return pltpu.CompilerParams(
        dimension_semantics=("arbitrary",) * n_grid, vmem_limit_bytes=VMEM_LIMIT)


def _prep_specs(w):
    rows, cols = w.shape
    chunk = (rows // N_PREP, cols)
    index = lambda i: (jnp.minimum(i, N_PREP - 1), 0)
    return (pl.BlockSpec(chunk, index), pl.BlockSpec(chunk, index),
            jax.ShapeDtypeStruct(w.shape, BF16), pltpu.VMEM(w.shape, BF16))


def _prep_weights(step, f32_refs, out_refs, resident_refs):
    for f_ref, o_ref, r_ref in zip(f32_refs, out_refs, resident_refs):
        rows = f_ref.shape[0]
        chunk = f_ref[...].astype(BF16)
        o_ref[...] = chunk
        r_ref[pl.ds(pl.multiple_of(step * rows, rows), rows), :] = chunk


def _token_step(i, n_t):
    j = jnp.maximum(i - N_PREP, 0)
    return j // n_t, j % n_t


def _pool_tail(pooled_groups, z, x, wgrp_ref, scale_ref, wout_ref, fin_ref, final_norm):
    gw = pooled_groups[0].shape[1]
    mixed = [_dot(p.astype(BF16), wgrp_ref[gi * gw:(gi + 1) * gw, :]) for gi, p in enumerate(pooled_groups)]
    mixed = jnp.concatenate(mixed, axis=1)
    gated = mixed * scale_ref[...] * _silu(z)
    y = _dot(gated.astype(BF16), wout_ref[...]) + x
    if final_norm:
        y = _rmsnorm(y, fin_ref[...])
    return y


def _pool_prompt_body(x_ref, gain_ref, scale_ref, fin_ref, winf_ref, wgrpf_ref, woutf_ref,
                      y_ref, st_ref, wino_ref, wgrpo_ref, wouto_ref,
                      win_ref, wgrp_ref, wout_ref, hist_ref, *, tm, n_t, final_norm):
    i = pl.program_id(0)

    @pl.when(i < N_PREP)
    def _():
        _prep_weights(i, (winf_ref, wgrpf_ref, woutf_ref), (wino_ref, wgrpo_ref, wouto_ref),
                      (win_ref, wgrp_ref, wout_ref))

    @pl.when(i >= N_PREP)
    def _():
        _pool_prompt_tile(_token_step(i, n_t)[1], x_ref, gain_ref, scale_ref, fin_ref, y_ref, st_ref,
                          win_ref, wgrp_ref, wout_ref, hist_ref, tm=tm, n_t=n_t, final_norm=final_norm)


def _pool_prompt_tile(t, x_ref, gain_ref, scale_ref, fin_ref, y_ref, st_ref,
                      win_ref, wgrp_ref, wout_ref, hist_ref, *, tm, n_t, final_norm):
    dp = scale_ref.shape[-1]
    gw = dp // len(POOL_WINDOWS)

    @pl.when(t == 0)
    def _():
        hist_ref[...] = jnp.zeros_like(hist_ref)

    x = x_ref[...]
    h = _rmsnorm(x, gain_ref[...]).astype(BF16)
    xz = _dot(h, win_ref[...])
    xb = xz[:, :dp]
    z = xz[:, dp:]
    ext = jnp.concatenate([hist_ref[...], xb], axis=0)
    last = xb[tm - POOL_HIST_PAD:, :]
    hist_ref[...] = last
    pos1 = lax.broadcasted_iota(jnp.int32, (tm, LANES), 0) + (t * tm + 1)
    pooled = []
    for gi, w in enumerate(POOL_WINDOWS):
        e = ext[:, gi * gw:(gi + 1) * gw]
        s = e + pltpu.roll(e, 1, 0)
        sh = 2
        while sh < w:
            s = s + pltpu.roll(s, sh, 0)
            sh *= 2
        s = s[POOL_HIST_PAD:, :]
        inv = 1.0 / jnp.minimum(pos1, w).astype(F32)
        inv = jnp.concatenate([inv] * (gw // LANES), axis=1)
        pooled.append(s * inv - xb[:, gi * gw:(gi + 1) * gw])
    y_ref[...] = _pool_tail(pooled, z, x, wgrp_ref, scale_ref, wout_ref, fin_ref, final_norm)

    @pl.when(t == n_t - 1)
    def _():
        st_ref[...] = last


def _pool_prompt(x, gain, scale, fin, win, wgrp, wout, *, tm, final_norm):
    b, t, d = x.shape
    dp = scale.shape[-1]
    n_t = t // tm
    body = functools.partial(_pool_prompt_body, tm=tm, n_t=n_t, final_norm=final_norm)
    w_in_specs, w_out_specs, w_out_shapes, w_scratch = zip(*[_prep_specs(w) for w in (win, wgrp, wout)])
    tok = lambda i: _token_step(i, n_t) + (0,)
    return pl.pallas_call(
        body,
        grid=(N_PREP + b * n_t,),
        in_specs=[pl.BlockSpec((None, tm, d), tok), _const_spec(gain.shape), _const_spec(scale.shape),
                  _const_spec(fin.shape), *w_in_specs],
        out_specs=[pl.BlockSpec((None, tm, d), tok),
                   pl.BlockSpec((None, POOL_HIST_PAD, dp), lambda i: (_token_step(i, n_t)[0], 0, 0)),
                   *w_out_specs],
        out_shape=[jax.ShapeDtypeStruct((b, t, d), F32),
                   jax.ShapeDtypeStruct((b, POOL_HIST_PAD, dp), F32), *w_out_shapes],
        scratch_shapes=[*w_scratch, pltpu.VMEM((POOL_HIST_PAD, dp), F32)],
        compiler_params=_params(1),
        name="pool_prompt",
    )(x, gain, scale, fin, win, wgrp, wout)


def _pool_sample_body(x_ref, st_ref, gain_ref, win_ref, wgrp_ref, scale_ref, wout_ref, fin_ref,
                      y_ref, nst_ref, *, final_norm):
    dp = scale_ref.shape[-1]
    gw = dp // len(POOL_WINDOWS)
    x = _load_rows(x_ref)
    h = _rmsnorm(x, gain_ref[...]).astype(BF16)
    xz = _dot(h, win_ref[...])
    xb = xz[:, :dp]
    z = xz[:, dp:]
    pooled = []
    for gi, w in enumerate(POOL_WINDOWS):
        xg = xb[:, gi * gw:(gi + 1) * gw]
        s = xg
        for k in range(1, w):
            s = s + st_ref[POOL_HIST - k, :, gi * gw:(gi + 1) * gw]
        cnt = float(min(PAST_LEN + 1, w))
        pooled.append(s * (1.0 / cnt) - xg)
    _store_rows(y_ref, _pool_tail(pooled, z, x, wgrp_ref, scale_ref, wout_ref, fin_ref, final_norm))
    nst_ref[:POOL_HIST - 1] = st_ref[1:]
    nst_ref[POOL_HIST - 1] = xb


def _pool_sample(x, st, gain, win, wgrp, scale, wout, fin, *, final_norm, y_shape):
    body = functools.partial(_pool_sample_body, final_norm=final_norm)
    return pl.pallas_call(
        body,
        out_shape=[jax.ShapeDtypeStruct(y_shape, F32), jax.ShapeDtypeStruct(st.shape, F32)],
        compiler_params=_params(0),
        name="pool_sample",
    )(x, st, gain, win, wgrp, scale, wout, fin)


def _sgu_front(x, gain_ref, win_ref, lng_ref, lnb_ref):
    ds = lng_ref.shape[-1]
    h = _rmsnorm(x, gain_ref[...]).astype(BF16)
    uvz = _dot(h, win_ref[...])
    u = _gelu(uvz[:, :ds])
    v = _gelu(uvz[:, ds:2 * ds])
    z = uvz[:, 2 * ds:]
    mu = jnp.mean(v, axis=-1, keepdims=True)
    vc = v - mu
    var = jnp.mean(vc * vc, axis=-1, keepdims=True)
    vn = vc * lax.rsqrt(var + EPS) * lng_ref[...] + lnb_ref[...]
    return u, vn, z


def _sgu_prompt_body(x_ref, gain_ref, lng_ref, lnb_ref, ws_ref, bsb_ref, winf_ref, woutf_ref,
                     y_ref, wino_ref, wouto_ref, win_ref, wout_ref, *, tm):
    i = pl.program_id(0)

    @pl.when(i < N_PREP)
    def _():
        _prep_weights(i, (winf_ref, woutf_ref), (wino_ref, wouto_ref), (win_ref, wout_ref))

    @pl.when(i >= N_PREP)
    def _():
        _sgu_prompt_tile(x_ref, gain_ref, win_ref, lng_ref, lnb_ref, ws_ref, bsb_ref, wout_ref, y_ref, tm=tm)


def _sgu_prompt_tile(x_ref, gain_ref, win_ref, lng_ref, lnb_ref, ws_ref, bsb_ref, wout_ref,
                     y_ref, *, tm):
    ds = lng_ref.shape[-1]
    gw = ds // SGU_GROUPS
    x = x_ref[...]
    u, vn, z = _sgu_front(x, gain_ref, win_ref, lng_ref, lnb_ref)
    ri = lax.broadcasted_iota(jnp.int32, (SGU_CHUNK, SGU_CHUNK), 0)
    ci = lax.broadcasted_iota(jnp.int32, (SGU_CHUNK, SGU_CHUNK), 1)
    ws = [jnp.where(ri >= ci, ws_ref[g], 0.0).astype(BF16) for g in range(SGU_GROUPS)]
    vnb = vn.astype(BF16)
    bsb = bsb_ref[...]
    rows = []
    for n in range(tm // SGU_CHUNK):
        r0 = n * SGU_CHUNK
        cols = [_dot(ws[g], vnb[r0:r0 + SGU_CHUNK, g * gw:(g + 1) * gw]) for g in range(SGU_GROUPS)]
        rows.append(jnp.concatenate(cols, axis=1) + bsb)
    s = jnp.concatenate(rows, axis=0)
    gated = u * s * _silu(z)
    y_ref[...] = _dot(gated.astype(BF16), wout_ref[...]) + x


def _sgu_prompt(x, gain, lng, lnb, ws, bsb, win, wout, *, tm):
    n, d = x.shape
    body = functools.partial(_sgu_prompt_body, tm=tm)
    w_in_specs, w_out_specs, w_out_shapes, w_scratch = zip(*[_prep_specs(w) for w in (win, wout)])
    tok = lambda i: (jnp.maximum(i - N_PREP, 0), 0)
    return pl.pallas_call(
        body,
        grid=(N_PREP + n // tm,),
        in_specs=[pl.BlockSpec((tm, d), tok)] + [
            _const_spec(a.shape) for a in (gain, lng, lnb, ws, bsb)] + [*w_in_specs],
        out_specs=[pl.BlockSpec((tm, d), tok), *w_out_specs],
        out_shape=[jax.ShapeDtypeStruct((n, d), F32), *w_out_shapes],
        scratch_shapes=[*w_scratch],
        compiler_params=_params(1),
        name="sgu_prompt",
    )(x, gain, lng, lnb, ws, bsb, win, wout)


def _sgu_sample_body(x_ref, gain_ref, win_ref, lng_ref, lnb_ref, w00_ref, b0_ref, wout_ref,
                     y_ref, v_ref):
    x = x_ref[...]
    u, vn, z = _sgu_front(x, gain_ref, win_ref, lng_ref, lnb_ref)
    s = vn * w00_ref[...] + b0_ref[...]
    gated = u * s * _silu(z)
    y_ref[...] = _dot(gated.astype(BF16), wout_ref[...]) + x
    _store_rows(v_ref, vn)


def _sgu_sample(x, gain, win, lng, lnb, w00, b0, wout):
    nb, d = x.shape
    ds = lng.shape[-1]
    return pl.pallas_call(
        _sgu_sample_body,
        out_shape=[jax.ShapeDtypeStruct((nb, d), F32), jax.ShapeDtypeStruct((nb, 1, ds), F32)],
        compiler_params=_params(0),
        name="sgu_sample",
    )(x, gain, win, lng, lnb, w00, b0, wout)


def _gdn_gates(ab, alog_ref, dtb_ref):
    g = -jnp.exp(alog_ref[...]) * jax.nn.softplus(ab + dtb_ref[...])
    beta = jax.nn.sigmoid(ab)
    return g, beta


def _lane_pick(a, lane, idx):
    return jnp.sum(jnp.where(lane == idx, a, 0.0), axis=-1, keepdims=True)


def _gdn_heads(c, g_like, beta_all, emit):
    rows = c.shape[0]
    qk = DN_HEADS * DN_DK
    lane = lax.broadcasted_iota(jnp.int32, (rows, LANES), 1)
    for hh in range(DN_HEADS):
        qh = c[:, hh * DN_DK:(hh + 1) * DN_DK]
        kh = c[:, qk + hh * DN_DK:qk + (hh + 1) * DN_DK]
        vh = c[:, 2 * qk + hh * DN_DV:2 * qk + (hh + 1) * DN_DV]
        qn = qh * lax.rsqrt(jnp.sum(qh * qh, axis=-1, keepdims=True) + EPS) * (DN_DK ** -0.5)
        kn = kh * lax.rsqrt(jnp.sum(kh * kh, axis=-1, keepdims=True) + EPS)
        emit(hh, qn, kn, vh, _lane_pick(g_like, lane, hh), _lane_pick(beta_all, lane, DN_HEADS + hh))


def _gdn_proj_prompt_body(x_ref, gain_ref, win_ref, cw_ref, alog_ref, dtb_ref,
                          q_ref, k_ref, kb_ref, vb_ref, gam_ref, z_ref, cst_ref, hist_ref,
                          *, tm, n_t):
    t = pl.program_id(1)
    qk = DN_HEADS * DN_DK
    cch = 3 * qk

    @pl.when(t == 0)
    def _():
        hist_ref[...] = jnp.zeros_like(hist_ref)

    x = x_ref[...]
    h = _rmsnorm(x, gain_ref[...]).astype(BF16)
    proj = _dot(h, win_ref[...])
    qkv = proj[:, :cch]
    z_ref[...] = proj[:, cch:cch + qk]
    ab = proj[:, cch + qk:cch + qk + LANES]
    ext = jnp.concatenate([hist_ref[...], qkv], axis=0)
    last = qkv[tm - CONV_HIST_PAD:, :]
    hist_ref[...] = last
    cw = cw_ref[...]
    conv = ext * cw[DN_CONV - 1:DN_CONV, :]
    for j in range(1, DN_CONV):
        conv = conv + pltpu.roll(ext, j, 0) * cw[DN_CONV - 1 - j:DN_CONV - j, :]
    c = _silu(conv[CONV_HIST_PAD:, :])

    g, beta = _gdn_gates(ab, alog_ref, dtb_ref)
    rin = lax.broadcasted_iota(jnp.int32, (tm, LANES), 0) & (DN_CHUNK - 1)
    gam = g
    sh = 1
    while sh < DN_CHUNK:
        gam = gam + jnp.where(rin >= sh, pltpu.roll(gam, sh, 0), 0.0)
        sh *= 2

    def emit(hh, qn, kn, vh, g_col, b_col):
        sl = slice(hh * DN_DK, (hh + 1) * DN_DK)
        q_ref[:, sl] = qn.astype(BF16)
        k_ref[:, sl] = kn.astype(BF16)
        kb_ref[:, sl] = (b_col * kn).astype(BF16)
        vb_ref[:, sl] = (b_col * vh).astype(BF16)
        gam_ref[:, sl] = jnp.broadcast_to(g_col, (tm, DN_DK))

    _gdn_heads(c, gam, beta, emit)

    @pl.when(t == n_t - 1)
    def _():
        cst_ref[...] = last


def _gdn_proj_prompt(x, gain, win, cw, alog, dtb, *, tm):
    b, t, d = x.shape
    qk = DN_HEADS * DN_DK
    cch = 3 * qk
    n_t = t // tm
    body = functools.partial(_gdn_proj_prompt_body, tm=tm, n_t=n_t)
    tok = lambda i, j: (i, j, 0)
    return pl.pallas_call(
        body,
        grid=(b, n_t),
        in_specs=[pl.BlockSpec((None, tm, d), tok)] + [
            _const_spec(a.shape) for a in (gain, win, cw, alog, dtb)],
        out_specs=[pl.BlockSpec((None, tm, qk), tok)] * 6 + [
            pl.BlockSpec((None, CONV_HIST_PAD, cch), lambda i, j: (i, 0, 0))],
        out_shape=[jax.ShapeDtypeStruct((b, t, qk), BF16)] * 4 + [
            jax.ShapeDtypeStruct((b, t, qk), F32)] * 2 + [
            jax.ShapeDtypeStruct((b, CONV_HIST_PAD, cch), F32)],
        scratch_shapes=[pltpu.VMEM((CONV_HIST_PAD, cch), F32)],
        compiler_params=_params(2),
        name="gdn_proj_prompt",
    )(x, gain, win, cw, alog, dtb)


def _unit_lower_inverse_minus_eye(ms, in_base, at_level):
    bf = lambda xs: [x.astype(BF16) for x in xs]
    a = [jnp.where(in_base, m, 0.0) for m in ms]
    e = [-x for x in a]
    ab = bf(a)
    p = [_dot(x, x) for x in ab]
    for it in range(INV_BASE_BITS - 1):
        pb = bf(p)
        ep = [_dot(x, y) for x, y in zip(bf(e), pb)]
        e = [x + y + z for x, y, z in zip(e, p, ep)]
        if it < INV_BASE_BITS - 2:
            p = [_dot(x, x) for x in pb]
    for mask in at_level:
        low = [jnp.where(mask, m, 0.0) for m in ms]
        eb = bf(e)
        x = [lo + _dot(lb, y) for lo, lb, y in zip(low, bf(low), eb)]
        ex = [_dot(y, xb) for y, xb in zip(eb, bf(x))]
        e = [ei - (xi + yi) for ei, xi, yi in zip(e, x, ex)]
    return e


def _gdn_out(o_heads, z, x, og_ref, wout_ref):
    normed = [oh * lax.rsqrt(jnp.mean(oh * oh, axis=-1, keepdims=True) + EPS) * og_ref[...]
              for oh in o_heads]
    gated = jnp.concatenate(normed, axis=1) * _silu(z)
    return _dot(gated.astype(BF16), wout_ref[...]) + x


def _gdn_delta_prompt_body(q_ref, k_ref, kb_ref, vb_ref, gam_ref, z_ref, x_ref, og_ref, wout_ref,
                           y_ref, sfin_ref, s_ref, o_ref, *, tm, n_t):
    t = pl.program_id(1)

    @pl.when(t == 0)
    def _():
        s_ref[...] = jnp.zeros_like(s_ref)

    c = DN_CHUNK
    ri = lax.broadcasted_iota(jnp.int32, (c, c), 0)
    ci = lax.broadcasted_iota(jnp.int32, (c, c), 1)
    incl = ri >= ci
    strict = ri > ci
    top = ri ^ ci
    in_base = (top >> INV_BASE_BITS) == 0
    levels = range(INV_BASE_BITS, c.bit_length() - 1)
    at_level = [(top >> lv) == 1 for lv in levels]
    n_chunks = tm // c
    pairs = [(n, hh) for n in range(n_chunks) for hh in range(DN_HEADS)]

    def blk(ref, n, hh):
        return ref[n * c:(n + 1) * c, hh * DN_DK:(hh + 1) * DN_DK]

    kh = [blk(k_ref, *p) for p in pairs]
    kbh = [blk(kb_ref, *p) for p in pairs]
    gam = [blk(gam_ref, *p) for p in pairs]
    kk = [_dot_nt(a, b) for a, b in zip(kbh, kh)]
    dec = [jnp.exp(jnp.where(incl, g - g.T, -1e30)) for g in gam]
    m = [jnp.where(strict, a * d, 0.0) for a, d in zip(kk, dec)]
    e = _unit_lower_inverse_minus_eye(m, in_base, at_level)
    eg = [jnp.exp(g) for g in gam]
    rhs = [jnp.concatenate([blk(vb_ref, *p).astype(F32), a.astype(F32) * x], axis=1)
           for p, a, x in zip(pairs, kbh, eg)]
    uw = [r + _dot(x.astype(BF16), r.astype(BF16)) for r, x in zip(rhs, e)]
    qh = [blk(q_ref, *p) for p in pairs]
    qkm = [(_dot_nt(a, b) * d).astype(BF16) for a, b, d in zip(qh, kh, dec)]
    qd = [(a.astype(F32) * x).astype(BF16) for a, x in zip(qh, eg)]
    gl = [g[c - 1:c, :] for g in gam]
    kt = [(a.astype(F32) * jnp.exp(l - g)).astype(BF16) for a, l, g in zip(kh, gl, gam)]
    al = [jnp.exp(l) for l in gl]

    for n in range(n_chunks):
        idx = [n * DN_HEADS + hh for hh in range(DN_HEADS)]
        s_old = [s_ref[hh] for hh in range(DN_HEADS)]
        sb = [x.astype(BF16) for x in s_old]
        w = [uw[i][:, :DN_DV] - _dot(uw[i][:, DN_DV:].astype(BF16), y) for i, y in zip(idx, sb)]
        wb = [x.astype(BF16) for x in w]
        for hh, i in enumerate(idx):
            o_ref[n * c:(n + 1) * c, hh * DN_DV:(hh + 1) * DN_DV] = (
                _dot(qd[i], sb[hh]) + _dot(qkm[i], wb[hh]))
        for hh, i in enumerate(idx):
            s_ref[hh] = al[i] * s_old[hh] + _dot_tn(kt[i], wb[hh])
    o = o_ref[...]
    o_heads = [o[:, hh * DN_DV:(hh + 1) * DN_DV] for hh in range(DN_HEADS)]
    y_ref[...] = _gdn_out(o_heads, z_ref[...], x_ref[...], og_ref, wout_ref)

    @pl.when(t == n_t - 1)
    def _():
        sfin_ref[...] = s_ref[...]


def _gdn_delta_prompt(q, k, kb, vb, gam, z, x, og, wout, *, tm):
    b, t, d = x.shape
    qk = DN_HEADS * DN_DK
    n_t = t // tm
    body = functools.partial(_gdn_delta_prompt_body, tm=tm, n_t=n_t)
    tok = lambda i, j: (i, j, 0)
    return pl.pallas_call(
        body,
        grid=(b, n_t),
        in_specs=[pl.BlockSpec((None, tm, qk), tok)] * 6 + [pl.BlockSpec((None, tm, d), tok),
                  _const_spec(og.shape), _const_spec(wout.shape)],
        out_specs=[pl.BlockSpec((None, tm, d), tok),
                   pl.BlockSpec((None, DN_HEADS, DN_DK, DN_DV), lambda i, j: (i, 0, 0, 0))],
        out_shape=[jax.ShapeDtypeStruct((b, t, d), F32),
                   jax.ShapeDtypeStruct((b, DN_HEADS, DN_DK, DN_DV), F32)],
        scratch_shapes=[pltpu.VMEM((DN_HEADS, DN_DK, DN_DV), F32),
                        pltpu.VMEM((tm, DN_HEADS * DN_DV), F32)],
        compiler_params=_params(2),
        name="gdn_delta_prompt",
    )(q, k, kb, vb, gam, z, x, og, wout)


def _gdn_proj_sample_body(x_ref, cst_ref, gain_ref, win_ref, cw_ref, alog_ref, dtb_ref,
                          q_ref, k_ref, v_ref, eg_ref, beta_ref, z_ref, ncst_ref):
    qk = DN_HEADS * DN_DK
    cch = 3 * qk
    nb = x_ref.shape[0]
    x = x_ref[...]
    h = _rmsnorm(x, gain_ref[...]).astype(BF16)
    proj = _dot(h, win_ref[...])
    qkv = proj[:, :cch]
    z_ref[...] = proj[:, cch:cch + qk]
    ab = proj[:, cch + qk:cch + qk + LANES]
    cw = cw_ref[...]
    conv = qkv * cw[DN_CONV - 1:DN_CONV, :]
    for j in range(DN_CONV - 1):
        conv = conv + cst_ref[j] * cw[j:j + 1, :]
    c = _silu(conv)
    ncst_ref[:DN_CONV - 2] = cst_ref[1:]
    ncst_ref[DN_CONV - 2] = qkv
    g, beta = _gdn_gates(ab, alog_ref, dtb_ref)

    def emit(hh, qn, kn, vh, g_col, b_col):
        q_ref[:, hh, :] = qn
        k_ref[:, hh, :] = kn
        v_ref[:, hh, :] = vh
        eg_ref[:, hh, :] = jnp.broadcast_to(jnp.exp(g_col), (nb, DN_DK))
        beta_ref[:, hh, :] = jnp.broadcast_to(b_col, (nb, DN_DK))

    _gdn_heads(c, g, beta, emit)


def _gdn_proj_sample(x, cst, gain, win, cw, alog, dtb):
    nb, d = x.shape
    qk = DN_HEADS * DN_DK
    return pl.pallas_call(
        _gdn_proj_sample_body,
        out_shape=[jax.ShapeDtypeStruct((nb, DN_HEADS, DN_DK), F32)] * 5 + [
            jax.ShapeDtypeStruct((nb, qk), F32), jax.ShapeDtypeStruct(cst.shape, F32)],
        compiler_params=_params(0),
        name="gdn_proj_sample",
    )(x, cst, gain, win, cw, alog, dtb)


def _gdn_state_sample_body(q_ref, k_ref, v_ref, eg_ref, beta_ref, s_ref, o_ref, ns_ref, *, nblk):
    hk = DN_HEADS * DN_DK
    row = lax.broadcasted_iota(jnp.int32, (DN_HEADS, hk), 0)
    blk = lax.broadcasted_iota(jnp.int32, (DN_HEADS, hk), 1) // DN_DK
    own = row == blk

    def blockdiag(a):
        return jnp.where(own, jnp.concatenate([a] * DN_HEADS, axis=1), 0.0).astype(BF16)

    for r in range(nblk):
        s_old = s_ref[r]
        kbd = blockdiag(k_ref[r])
        eg = eg_ref[r]
        kts = _dot(kbd, s_old.reshape(hk, DN_DV).astype(BF16))
        w = beta_ref[r] * (v_ref[r] - eg * kts)
        outer = _dot_tn(kbd, w.astype(BF16))
        s_new = s_old * eg[:, None, :] + outer.reshape(DN_HEADS, DN_DK, DN_DV)
        ns_ref[r] = s_new
        o_ref[r] = _dot(blockdiag(q_ref[r]), s_new.reshape(hk, DN_DV).astype(BF16))


def _gdn_state_sample(q, k, v, eg, beta, s, *, nblk):
    nb = s.shape[0]
    body = functools.partial(_gdn_state_sample_body, nblk=nblk)
    vec = pl.BlockSpec((nblk, DN_HEADS, DN_DK), lambda i: (i, 0, 0))
    st = pl.BlockSpec((nblk, DN_HEADS, DN_DK, DN_DV), lambda i: (i, 0, 0, 0))
    return pl.pallas_call(
        body,
        grid=(nb // nblk,),
        in_specs=[vec] * 5 + [st],
        out_specs=[vec, st],
        out_shape=[jax.ShapeDtypeStruct((nb, DN_HEADS, DN_DV), F32),
                   jax.ShapeDtypeStruct(s.shape, F32)],
        compiler_params=_params(1),
        name="gdn_state_sample",
    )(q, k, v, eg, beta, s)


def _gdn_out_sample_body(o_ref, z_ref, x_ref, og_ref, wout_ref, y_ref):
    o_heads = [o_ref[:, hh, :] for hh in range(DN_HEADS)]
    y_ref[...] = _gdn_out(o_heads, z_ref[...], x_ref[...], og_ref, wout_ref)


def _gdn_out_sample(o, z, x, og, wout):
    return pl.pallas_call(
        _gdn_out_sample_body,
        out_shape=jax.ShapeDtypeStruct(x.shape, F32),
        compiler_params=_params(0),
        name="gdn_out_sample",
    )(o, z, x, og, wout)


def _row(a):
    return a.reshape(1, -1).astype(F32)


def _time_major(st):
    return jnp.transpose(st, (1, 0, 2))


def _forward(x_prompt, x_sample, state_pool_l0, state_conv_l2, state_delta_l2, state_pool_l3,
             l0_norm, l0_pool_w_in, l0_pool_w_grp, l0_pool_scale, l0_pool_w_out,
             l1_norm, l1_sgu_w_in, l1_sgu_ln_g, l1_sgu_ln_b, l1_sgu_w_s, l1_sgu_b_s, l1_sgu_w_out,
             l2_norm, l2_dn_w_in, l2_dn_conv_w, l2_dn_a_log, l2_dn_dt_bias, l2_dn_o_gain, l2_dn_w_out,
             l3_norm, l3_pool_w_in, l3_pool_w_grp, l3_pool_scale, l3_pool_w_out,
             final_norm, *, tm_pool, tm_sgu, tm_gdn_proj, tm_gdn, nblk_state):
    b, t, d = x_prompt.shape
    nb = x_sample.shape[0]
    dp = l0_pool_scale.shape[0]
    ds = l1_sgu_ln_g.shape[0]
    gw_s = ds // SGU_GROUPS
    qk = DN_HEADS * DN_DK
    cch = 3 * qk
    fin = _row(final_norm)
    xs = x_sample

    def pool_layer(xp, xs_, st, norm, w_in, w_grp, scale, w_out, final):
        gain, sc = _row(norm), _row(scale)
        yp, st_p, win_b, wgrp_b, wout_b = _pool_prompt(
            xp, gain, sc, fin, w_in, w_grp.reshape(-1, w_grp.shape[-1]), w_out,
            tm=tm_pool, final_norm=final)
        ys, st_s = _pool_sample(xs_, _time_major(st), gain, win_b, wgrp_b, sc, wout_b, fin,
                                final_norm=final, y_shape=(nb, 1, d) if final else (nb, d))
        return yp, ys, st_p[:, POOL_HIST_PAD - POOL_HIST:, :], _time_major(st_s)

    xp, xs, pool0_p, pool0_s = pool_layer(
        x_prompt, xs, state_pool_l0, l0_norm, l0_pool_w_in, l0_pool_w_grp, l0_pool_scale,
        l0_pool_w_out, False)

    bsb = jnp.repeat(l1_sgu_b_s.T.astype(F32), gw_s, axis=1)
    xp, sgu_w_in, sgu_w_out = _sgu_prompt(
        xp.reshape(b * t, d), _row(l1_norm), _row(l1_sgu_ln_g), _row(l1_sgu_ln_b),
        l1_sgu_w_s.astype(F32), bsb, l1_sgu_w_in, l1_sgu_w_out, tm=tm_sgu)
    xp = xp.reshape(b, t, d)
    w00 = jnp.repeat(l1_sgu_w_s[:, 0, 0].astype(F32), gw_s)[None, :]
    b0 = jnp.repeat(l1_sgu_b_s[:, 0].astype(F32), gw_s)[None, :]
    xs, sgu1_s = _sgu_sample(xs, _row(l1_norm), sgu_w_in, _row(l1_sgu_ln_g), _row(l1_sgu_ln_b),
                             w00, b0, sgu_w_out)

    n_proj = l2_dn_w_in.shape[1]
    n_pad = cch + qk + LANES
    dn_w_in = jnp.pad(l2_dn_w_in, ((0, 0), (0, n_pad - n_proj))).astype(BF16)
    dn_w_out = l2_dn_w_out.astype(BF16)
    alog = jnp.pad(l2_dn_a_log.astype(F32), (0, LANES - DN_HEADS))[None, :]
    dtb = jnp.pad(l2_dn_dt_bias.astype(F32), (0, LANES - DN_HEADS))[None, :]
    cw = l2_dn_conv_w.astype(F32)
    og = _row(l2_dn_o_gain)
    q, k, kb, vb, gam, z, conv2_p = _gdn_proj_prompt(xp, _row(l2_norm), dn_w_in, cw, alog, dtb,
                                                     tm=tm_gdn_proj)
    xp, dn2_p = _gdn_delta_prompt(q, k, kb, vb, gam, z, xp, og, dn_w_out, tm=tm_gdn)
    conv2_p = conv2_p[:, CONV_HIST_PAD - (DN_CONV - 1):, :]

    qs, ks, vs, egs, betas, zs, conv2_s = _gdn_proj_sample(
        xs, _time_major(state_conv_l2), _row(l2_norm), dn_w_in, cw, alog, dtb)
    conv2_s = _time_major(conv2_s)
    o_s, dn2_s = _gdn_state_sample(qs, ks, vs, egs, betas, state_delta_l2, nblk=nblk_state)
    xs = _gdn_out_sample(o_s, zs, xs, og, dn_w_out)

    yp, ys, pool3_p, pool3_s = pool_layer(
        xp, xs, state_pool_l3, l3_norm, l3_pool_w_in, l3_pool_w_grp, l3_pool_scale,
        l3_pool_w_out, True)

    return (yp, ys, pool0_p, pool0_s, sgu1_s,
            conv2_p, conv2_s, dn2_p, dn2_s, pool3_p, pool3_s)


def kernel(x_prompt, x_sample, state_pool_l0, state_conv_l2, state_delta_l2, state_pool_l3, l0_norm, l0_pool_w_in, l0_pool_w_grp, l0_pool_scale, l0_pool_w_out, l1_norm, l1_sgu_w_in, l1_sgu_ln_g, l1_sgu_ln_b, l1_sgu_w_s, l1_sgu_b_s, l1_sgu_w_out, l2_norm, l2_dn_w_in, l2_dn_conv_w, l2_dn_a_log, l2_dn_dt_bias, l2_dn_o_gain, l2_dn_w_out, l3_norm, l3_pool_w_in, l3_pool_w_grp, l3_pool_scale, l3_pool_w_out, final_norm):
    return _forward(
        x_prompt, x_sample, state_pool_l0, state_conv_l2, state_delta_l2, state_pool_l3,
        l0_norm, l0_pool_w_in, l0_pool_w_grp, l0_pool_scale, l0_pool_w_out,
        l1_norm, l1_sgu_w_in, l1_sgu_ln_g, l1_sgu_ln_b, l1_sgu_w_s, l1_sgu_b_s, l1_sgu_w_out,
        l2_norm, l2_dn_w_in, l2_dn_conv_w, l2_dn_a_log, l2_dn_dt_bias, l2_dn_o_gain, l2_dn_w_out,
        l3_norm, l3_pool_w_in, l3_pool_w_grp, l3_pool_scale, l3_pool_w_out,
        final_norm, tm_pool=512, tm_sgu=512, tm_gdn_proj=512, tm_gdn=256, nblk_state=8)
```

```python
import functools

import jax
import jax.numpy as jnp
from jax import lax
from jax.experimental import pallas as pl
from jax.experimental.pallas import tpu as pltpu

F32 = jnp.float32
BF16 = jnp.bfloat16
EPS = 1e-6
PAST_LEN = 16384
POOL_WINDOWS = (2, 4, 8, 16)
POOL_HIST = 15
POOL_HIST_PAD = 16
SGU_CHUNK = 128
SGU_GROUPS = 4
DN_HEADS = 8
DN_DK = 128
DN_DV = 128
DN_CONV = 4
CONV_HIST_PAD = 8
DN_CHUNK = 128
INV_BASE_BITS = 4
LANES = 128
N_PREP = 8
VMEM_LIMIT = 60 * 1024 * 1024


def _rmsnorm(x, gain):
    ms = jnp.mean(x * x, axis=-1, keepdims=True)
    return x * lax.rsqrt(ms + EPS) * gain


def _silu(z):
    return z * jax.nn.sigmoid(z)


def _gelu(x):
    return 0.5 * x * (1.0 + lax.erf(x * (2.0 ** -0.5)))


def _dot(a, b):
    return jnp.dot(a, b, preferred_element_type=F32)


def _dot_nt(a, b):
    return lax.dot_general(a, b, (((1,), (1,)), ((), ())), preferred_element_type=F32)


def _dot_tn(a, b):
    return lax.dot_general(a, b, (((0,), (0,)), ((), ())), preferred_element_type=F32)


def _load_rows(ref):
    return ref[:, 0, :] if len(ref.shape) == 3 else ref[...]


def _store_rows(ref, val):
    if len(ref.shape) == 3:
        ref[:, 0, :] = val
    else:
        ref[...] = val


def _const_spec(shape):
    nd = len(shape)
    return pl.BlockSpec(shape, lambda *_: (0,) * nd, pipeline_mode=pl.Buffered(1))


def _params(n_grid):
    return pltpu.CompilerParams(
        dimension_semantics=("arbitrary",) * n_grid, vmem_limit_bytes=VMEM_LIMIT)


def _prep_specs(w):
    rows, cols = w.shape
    chunk = (rows // N_PREP, cols)
    index = lambda i: (jnp.minimum(i, N_PREP - 1), 0)
    return (pl.BlockSpec(chunk, index), pl.BlockSpec(chunk, index),
            jax.ShapeDtypeStruct(w.shape, BF16), pltpu.VMEM(w.shape, BF16))


def _prep_weights(step, f32_refs, out_refs, resident_refs):
    for f_ref, o_ref, r_ref in zip(f32_refs, out_refs, resident_refs):
        rows = f_ref.shape[0]
        chunk = f_ref[...].astype(BF16)
        o_ref[...] = chunk
        r_ref[pl.ds(pl.multiple_of(step * rows, rows), rows), :] = chunk


def _token_step(i, n_t):
    j = jnp.maximum(i - N_PREP, 0)
    return j // n_t, j % n_t


def _pool_tail(pooled_groups, z, x, wgrp_ref, scale_ref, wout_ref, fin_ref, final_norm):
    gw = pooled_groups[0].shape[1]
    mixed = [_dot(p.astype(BF16), wgrp_ref[gi * gw:(gi + 1) * gw, :]) for gi, p in enumerate(pooled_groups)]
    mixed = jnp.concatenate(mixed, axis=1)
    gated = mixed * scale_ref[...] * _silu(z)
    y = _dot(gated.astype(BF16), wout_ref[...]) + x
    if final_norm:
        y = _rmsnorm(y, fin_ref[...])
    return y


def _pool_prompt_body(x_ref, gain_ref, scale_ref, fin_ref, winf_ref, wgrpf_ref, woutf_ref,
                      y_ref, st_ref, wino_ref, wgrpo_ref, wouto_ref,
                      win_ref, wgrp_ref, wout_ref, hist_ref, *, tm, n_t, final_norm):
    i = pl.program_id(0)

    @pl.when(i < N_PREP)
    def _():
        _prep_weights(i, (winf_ref, wgrpf_ref, woutf_ref), (wino_ref, wgrpo_ref, wouto_ref),
                      (win_ref, wgrp_ref, wout_ref))

    @pl.when(i >= N_PREP)
    def _():
        _pool_prompt_tile(_token_step(i, n_t)[1], x_ref, gain_ref, scale_ref, fin_ref, y_ref, st_ref,
                          win_ref, wgrp_ref, wout_ref, hist_ref, tm=tm, n_t=n_t, final_norm=final_norm)


def _pool_prompt_tile(t, x_ref, gain_ref, scale_ref, fin_ref, y_ref, st_ref,
                      win_ref, wgrp_ref, wout_ref, hist_ref, *, tm, n_t, final_norm):
    dp = scale_ref.shape[-1]
    gw = dp // len(POOL_WINDOWS)

    @pl.when(t == 0)
    def _():
        hist_ref[...] = jnp.zeros_like(hist_ref)

    x = x_ref[...]
    h = _rmsnorm(x, gain_ref[...]).astype(BF16)
    xz = _dot(h, win_ref[...])
    xb = xz[:, :dp]
    z = xz[:, dp:]
    ext = jnp.concatenate([hist_ref[...], xb], axis=0)
    last = xb[tm - POOL_HIST_PAD:, :]
    hist_ref[...] = last
    pos1 = lax.broadcasted_iota(jnp.int32, (tm, LANES), 0) + (t * tm + 1)
    pooled = []
    for gi, w in enumerate(POOL_WINDOWS):
        e = ext[:, gi * gw:(gi + 1) * gw]
        s = e + pltpu.roll(e, 1, 0)
        sh = 2
        while sh < w:
            s = s + pltpu.roll(s, sh, 0)
            sh *= 2
        s = s[POOL_HIST_PAD:, :]
        inv = 1.0 / jnp.minimum(pos1, w).astype(F32)
        inv = jnp.concatenate([inv] * (gw // LANES), axis=1)
        pooled.append(s * inv - xb[:, gi * gw:(gi + 1) * gw])
    y_ref[...] = _pool_tail(pooled, z, x, wgrp_ref, scale_ref, wout_ref, fin_ref, final_norm)

    @pl.when(t == n_t - 1)
    def _():
        st_ref[...] = last


def _pool_prompt(x, gain, scale, fin, win, wgrp, wout, *, tm, final_norm):
    b, t, d = x.shape
    dp = scale.shape[-1]
    n_t = t // tm
    body = functools.partial(_pool_prompt_body, tm=tm, n_t=n_t, final_norm=final_norm)
    w_in_specs, w_out_specs, w_out_shapes, w_scratch = zip(*[_prep_specs(w) for w in (win, wgrp, wout)])
    tok = lambda i: _token_step(i, n_t) + (0,)
    return pl.pallas_call(
        body,
        grid=(N_PREP + b * n_t,),
        in_specs=[pl.BlockSpec((None, tm, d), tok), _const_spec(gain.shape), _const_spec(scale.shape),
                  _const_spec(fin.shape), *w_in_specs],
        out_specs=[pl.BlockSpec((None, tm, d), tok),
                   pl.BlockSpec((None, POOL_HIST_PAD, dp), lambda i: (_token_step(i, n_t)[0], 0, 0)),
                   *w_out_specs],
        out_shape=[jax.ShapeDtypeStruct((b, t, d), F32),
                   jax.ShapeDtypeStruct((b, POOL_HIST_PAD, dp), F32), *w_out_shapes],
        scratch_shapes=[*w_scratch, pltpu.VMEM((POOL_HIST_PAD, dp), F32)],
        compiler_params=_params(1),
        name="pool_prompt",
    )(x, gain, scale, fin, win, wgrp, wout)


def _pool_sample_body(x_ref, st_ref, gain_ref, win_ref, wgrp_ref, scale_ref, wout_ref, fin_ref,
                      y_ref, nst_ref, *, final_norm):
    dp = scale_ref.shape[-1]
    gw = dp // len(POOL_WINDOWS)
    x = _load_rows(x_ref)
    h = _rmsnorm(x, gain_ref[...]).astype(BF16)
    xz = _dot(h, win_ref[...])
    xb = xz[:, :dp]
    z = xz[:, dp:]
    pooled = []
    for gi, w in enumerate(POOL_WINDOWS):
        xg = xb[:, gi * gw:(gi + 1) * gw]
        s = xg
        for k in range(1, w):
            s = s + st_ref[POOL_HIST - k, :, gi * gw:(gi + 1) * gw]
        cnt = float(min(PAST_LEN + 1, w))
        pooled.append(s * (1.0 / cnt) - xg)
    _store_rows(y_ref, _pool_tail(pooled, z, x, wgrp_ref, scale_ref, wout_ref, fin_ref, final_norm))
    nst_ref[:POOL_HIST - 1] = st_ref[1:]
    nst_ref[POOL_HIST - 1] = xb


def _pool_sample(x, st, gain, win, wgrp, scale, wout, fin, *, final_norm, y_shape):
    body = functools.partial(_pool_sample_body, final_norm=final_norm)
    return pl.pallas_call(
        body,
        out_shape=[jax.ShapeDtypeStruct(y_shape, F32), jax.ShapeDtypeStruct(st.shape, F32)],
        compiler_params=_params(0),
        name="pool_sample",
    )(x, st, gain, win, wgrp, scale, wout, fin)


def _sgu_front(x, gain_ref, win_ref, lng_ref, lnb_ref):
    ds = lng_ref.shape[-1]
    h = _rmsnorm(x, gain_ref[...]).astype(BF16)
    uvz = _dot(h, win_ref[...])
    u = _gelu(uvz[:, :ds])
    v = _gelu(uvz[:, ds:2 * ds])
    z = _silu(uvz[:, 2 * ds:])
    mu = jnp.mean(v, axis=-1, keepdims=True)
    vc = v - mu
    var = jnp.mean(vc * vc, axis=-1, keepdims=True)
    vn = vc * lax.rsqrt(var + EPS) * lng_ref[...] + lnb_ref[...]
    return u, vn, z


def _sgu_prompt_body(x_ref, gain_ref, lng_ref, lnb_ref, ws_ref, bsb_ref, winf_ref, woutf_ref,
                     y_ref, wino_ref, wouto_ref, win_ref, wout_ref, *, tm):
    i = pl.program_id(0)

    @pl.when(i < N_PREP)
    def _():
        _prep_weights(i, (winf_ref, woutf_ref), (wino_ref, wouto_ref), (win_ref, wout_ref))

    @pl.when(i >= N_PREP)
    def _():
        _sgu_prompt_tile(x_ref, gain_ref, win_ref, lng_ref, lnb_ref, ws_ref, bsb_ref, wout_ref, y_ref, tm=tm)


def _sgu_prompt_tile(x_ref, gain_ref, win_ref, lng_ref, lnb_ref, ws_ref, bsb_ref, wout_ref,
                     y_ref, *, tm):
    ds = lng_ref.shape[-1]
    gw = ds // SGU_GROUPS
    x = x_ref[...]
    u, vn, z = _sgu_front(x, gain_ref, win_ref, lng_ref, lnb_ref)
    ri = lax.broadcasted_iota(jnp.int32, (SGU_CHUNK, SGU_CHUNK), 0)
    ci = lax.broadcasted_iota(jnp.int32, (SGU_CHUNK, SGU_CHUNK), 1)
    ws = [jnp.where(ri >= ci, ws_ref[g], 0.0).astype(BF16) for g in range(SGU_GROUPS)]
    vnb = vn.astype(BF16)
    bsb = bsb_ref[...]
    rows = []
    for n in range(tm // SGU_CHUNK):
        r0 = n * SGU_CHUNK
        cols = [_dot(ws[g], vnb[r0:r0 + SGU_CHUNK, g * gw:(g + 1) * gw]) for g in range(SGU_GROUPS)]
        rows.append(jnp.concatenate(cols, axis=1) + bsb)
    s = jnp.concatenate(rows, axis=0)
    gated = u * s * z
    y_ref[...] = _dot(gated.astype(BF16), wout_ref[...]) + x


def _sgu_prompt(x, gain, lng, lnb, ws, bsb, win, wout, *, tm):
    n, d = x.shape
    body = functools.partial(_sgu_prompt_body, tm=tm)
    w_in_specs, w_out_specs, w_out_shapes, w_scratch = zip(*[_prep_specs(w) for w in (win, wout)])
    tok = lambda i: (jnp.maximum(i - N_PREP, 0), 0)
    return pl.pallas_call(
        body,
        grid=(N_PREP + n // tm,),
        in_specs=[pl.BlockSpec((tm, d), tok)] + [
            _const_spec(a.shape) for a in (gain, lng, lnb, ws, bsb)] + [*w_in_specs],
        out_specs=[pl.BlockSpec((tm, d), tok), *w_out_specs],
        out_shape=[jax.ShapeDtypeStruct((n, d), F32), *w_out_shapes],
        scratch_shapes=[*w_scratch],
        compiler_params=_params(1),
        name="sgu_prompt",
    )(x, gain, lng, lnb, ws, bsb, win, wout)


def _sgu_sample_body(x_ref, gain_ref, win_ref, lng_ref, lnb_ref, w00_ref, b0_ref, wout_ref,
                     y_ref, v_ref):
    x = x_ref[...]
    u, vn, z = _sgu_front(x, gain_ref, win_ref, lng_ref, lnb_ref)
    s = vn * w00_ref[...] + b0_ref[...]
    gated = u * s * z
    y_ref[...] = _dot(gated.astype(BF16), wout_ref[...]) + x
    _store_rows(v_ref, vn)


def _sgu_sample(x, gain, win, lng, lnb, w00, b0, wout):
    nb, d = x.shape
    ds = lng.shape[-1]
    return pl.pallas_call(
        _sgu_sample_body,
        out_shape=[jax.ShapeDtypeStruct((nb, d), F32), jax.ShapeDtypeStruct((nb, 1, ds), F32)],
        compiler_params=_params(0),
        name="sgu_sample",
    )(x, gain, win, lng, lnb, w00, b0, wout)


def _gdn_gates(ab, alog_ref, dtb_ref):
    g = -jnp.exp(alog_ref[...]) * jax.nn.softplus(ab + dtb_ref[...])
    beta = jax.nn.sigmoid(ab)
    return g, beta


def _lane_pick(a, lane, idx):
    return jnp.sum(jnp.where(lane == idx, a, 0.0), axis=-1, keepdims=True)


def _gdn_heads(c, g_like, beta_all, emit):
    rows = c.shape[0]
    qk = DN_HEADS * DN_DK
    lane = lax.broadcasted_iota(jnp.int32, (rows, LANES), 1)
    for hh in range(DN_HEADS):
        qh = c[:, hh * DN_DK:(hh + 1) * DN_DK]
        kh = c[:, qk + hh * DN_DK:qk + (hh + 1) * DN_DK]
        vh = c[:, 2 * qk + hh * DN_DV:2 * qk + (hh + 1) * DN_DV]
        qn = qh * lax.rsqrt(jnp.sum(qh * qh, axis=-1, keepdims=True) + EPS) * (DN_DK ** -0.5)
        kn = kh * lax.rsqrt(jnp.sum(kh * kh, axis=-1, keepdims=True) + EPS)
        emit(hh, qn, kn, vh, _lane_pick(g_like, lane, hh), _lane_pick(beta_all, lane, DN_HEADS + hh))


def _gdn_proj_prompt_body(x_ref, gain_ref, win_ref, cw_ref, alog_ref, dtb_ref,
                          q_ref, k_ref, kb_ref, vb_ref, gam_ref, z_ref, cst_ref, hist_ref,
                          *, tm, n_t):
    t = pl.program_id(1)
    qk = DN_HEADS * DN_DK
    cch = 3 * qk

    @pl.when(t == 0)
    def _():
        hist_ref[...] = jnp.zeros_like(hist_ref)

    x = x_ref[...]
    h = _rmsnorm(x, gain_ref[...]).astype(BF16)
    proj = _dot(h, win_ref[...])
    qkv = proj[:, :cch]
    z_ref[...] = proj[:, cch:cch + qk]
    ab = proj[:, cch + qk:cch + qk + LANES]
    ext = jnp.concatenate([hist_ref[...], qkv], axis=0)
    last = qkv[tm - CONV_HIST_PAD:, :]
    hist_ref[...] = last
    cw = cw_ref[...]
    conv = ext * cw[DN_CONV - 1:DN_CONV, :]
    for j in range(1, DN_CONV):
        conv = conv + pltpu.roll(ext, j, 0) * cw[DN_CONV - 1 - j:DN_CONV - j, :]
    c = _silu(conv[CONV_HIST_PAD:, :])

    g, beta = _gdn_gates(ab, alog_ref, dtb_ref)
    rin = lax.broadcasted_iota(jnp.int32, (tm, LANES), 0) & (DN_CHUNK - 1)
    gam = g
    sh = 1
    while sh < DN_CHUNK:
        gam = gam + jnp.where(rin >= sh, pltpu.roll(gam, sh, 0), 0.0)
        sh *= 2

    def emit(hh, qn, kn, vh, g_col, b_col):
        sl = slice(hh * DN_DK, (hh + 1) * DN_DK)
        q_ref[:, sl] = qn.astype(BF16)
        k_ref[:, sl] = kn.astype(BF16)
        kb_ref[:, sl] = (b_col * kn).astype(BF16)
        vb_ref[:, sl] = (b_col * vh).astype(BF16)
        gam_ref[:, sl] = jnp.broadcast_to(g_col, (tm, DN_DK))

    _gdn_heads(c, gam, beta, emit)

    @pl.when(t == n_t - 1)
    def _():
        cst_ref[...] = last


def _gdn_proj_prompt(x, gain, win, cw, alog, dtb, *, tm):
    b, t, d = x.shape
    qk = DN_HEADS * DN_DK
    cch = 3 * qk
    n_t = t // tm
    body = functools.partial(_gdn_proj_prompt_body, tm=tm, n_t=n_t)
    tok = lambda i, j: (i, j, 0)
    return pl.pallas_call(
        body,
        grid=(b, n_t),
        in_specs=[pl.BlockSpec((None, tm, d), tok)] + [
            _const_spec(a.shape) for a in (gain, win, cw, alog, dtb)],
        out_specs=[pl.BlockSpec((None, tm, qk), tok)] * 6 + [
            pl.BlockSpec((None, CONV_HIST_PAD, cch), lambda i, j: (i, 0, 0))],
        out_shape=[jax.ShapeDtypeStruct((b, t, qk), BF16)] * 4 + [
            jax.ShapeDtypeStruct((b, t, qk), F32)] * 2 + [
            jax.ShapeDtypeStruct((b, CONV_HIST_PAD, cch), F32)],
        scratch_shapes=[pltpu.VMEM((CONV_HIST_PAD, cch), F32)],
        compiler_params=_params(2),
        name="gdn_proj_prompt",
    )(x, gain, win, cw, alog, dtb)


def _unit_lower_inverse_minus_eye(ms, in_base, at_level):
    bf = lambda xs: [x.astype(BF16) for x in xs]
    a = [jnp.where(in_base, m, 0.0) for m in ms]
    e = [-x for x in a]
    ab = bf(a)
    p = [_dot(x, x) for x in ab]
    for it in range(INV_BASE_BITS - 1):
        pb = bf(p)
        ep = [_dot(x, y) for x, y in zip(bf(e), pb)]
        e = [x + y + z for x, y, z in zip(e, p, ep)]
        if it < INV_BASE_BITS - 2:
            p = [_dot(x, x) for x in pb]
    for mask in at_level:
        low = [jnp.where(mask, m, 0.0) for m in ms]
        eb = bf(e)
        x = [lo + _dot(lb, y) for lo, lb, y in zip(low, bf(low), eb)]
        ex = [_dot(y, xb) for y, xb in zip(eb, bf(x))]
        e = [ei - (xi + yi) for ei, xi, yi in zip(e, x, ex)]
    return e


def _gdn_out(o_heads, z, x, og_ref, wout_ref):
    normed = [oh * lax.rsqrt(jnp.mean(oh * oh, axis=-1, keepdims=True) + EPS) * og_ref[...]
              for oh in o_heads]
    gated = jnp.concatenate(normed, axis=1) * _silu(z)
    return _dot(gated.astype(BF16), wout_ref[...]) + x


def _gdn_delta_prompt_body(q_ref, k_ref, kb_ref, vb_ref, gam_ref, z_ref, x_ref, og_ref, wout_ref,
                           y_ref, sfin_ref, s_ref, o_ref, *, tm, n_t):
    t = pl.program_id(1)

    @pl.when(t == 0)
    def _():
        s_ref[...] = jnp.zeros_like(s_ref)

    c = DN_CHUNK
    ri = lax.broadcasted_iota(jnp.int32, (c, c), 0)
    ci = lax.broadcasted_iota(jnp.int32, (c, c), 1)
    incl = ri >= ci
    strict = ri > ci
    top = ri ^ ci
    in_base = (top >> INV_BASE_BITS) == 0
    levels = range(INV_BASE_BITS, c.bit_length() - 1)
    at_level = [(top >> lv) == 1 for lv in levels]
    n_chunks = tm // c
    pairs = [(n, hh) for n in range(n_chunks) for hh in range(DN_HEADS)]

    def blk(ref, n, hh):
        return ref[n * c:(n + 1) * c, hh * DN_DK:(hh + 1) * DN_DK]

    kh = [blk(k_ref, *p) for p in pairs]
    kbh = [blk(kb_ref, *p) for p in pairs]
    gam = [blk(gam_ref, *p) for p in pairs]
    kk = [_dot_nt(a, b) for a, b in zip(kbh, kh)]
    dec = [jnp.exp(jnp.where(incl, g - g.T, -1e30)) for g in gam]
    m = [jnp.where(strict, a * d, 0.0) for a, d in zip(kk, dec)]
    e = _unit_lower_inverse_minus_eye(m, in_base, at_level)
    eg = [jnp.exp(g) for g in gam]
    rhs = [jnp.concatenate([blk(vb_ref, *p).astype(F32), a.astype(F32) * x], axis=1)
           for p, a, x in zip(pairs, kbh, eg)]
    uw = [r + _dot(x.astype(BF16), r.astype(BF16)) for r, x in zip(rhs, e)]
    qh = [blk(q_ref, *p) for p in pairs]
    qkm = [(_dot_nt(a, b) * d).astype(BF16) for a, b, d in zip(qh, kh, dec)]
    qd = [(a.astype(F32) * x).astype(BF16) for a, x in zip(qh, eg)]
    gl = [g[c - 1:c, :] for g in gam]
    kt = [(a.astype(F32) * jnp.exp(l - g)).astype(BF16) for a, l, g in zip(kh, gl, gam)]
    al = [jnp.exp(l) for l in gl]

    for n in range(n_chunks):
        idx = [n * DN_HEADS + hh for hh in range(DN_HEADS)]
        s_old = [s_ref[hh] for hh in range(DN_HEADS)]
        sb = [x.astype(BF16) for x in s_old]
        w = [uw[i][:, :DN_DV] - _dot(uw[i][:, DN_DV:].astype(BF16), y) for i, y in zip(idx, sb)]
        wb = [x.astype(BF16) for x in w]
        for hh, i in enumerate(idx):
            o_ref[n * c:(n + 1) * c, hh * DN_DV:(hh + 1) * DN_DV] = (
                _dot(qd[i], sb[hh]) + _dot(qkm[i], wb[hh]))
        for hh, i in enumerate(idx):
            s_ref[hh] = al[i] * s_old[hh] + _dot_tn(kt[i], wb[hh])
    o = o_ref[...]
    o_heads = [o[:, hh * DN_DV:(hh + 1) * DN_DV] for hh in range(DN_HEADS)]
    y_ref[...] = _gdn_out(o_heads, z_ref[...], x_ref[...], og_ref, wout_ref)

    @pl.when(t == n_t - 1)
    def _():
        sfin_ref[...] = s_ref[...]


def _gdn_delta_prompt(q, k, kb, vb, gam, z, x, og, wout, *, tm):
    b, t, d = x.shape
    qk = DN_HEADS * DN_DK
    n_t = t // tm
    body = functools.partial(_gdn_delta_prompt_body, tm=tm, n_t=n_t)
    tok = lambda i, j: (i, j, 0)
    return pl.pallas_call(
        body,
        grid=(b, n_t),
        in_specs=[pl.BlockSpec((None, tm, qk), tok)] * 6 + [pl.BlockSpec((None, tm, d), tok),
                  _const_spec(og.shape), _const_spec(wout.shape)],
        out_specs=[pl.BlockSpec((None, tm, d), tok),
                   pl.BlockSpec((None, DN_HEADS, DN_DK, DN_DV), lambda i, j: (i, 0, 0, 0))],
        out_shape=[jax.ShapeDtypeStruct((b, t, d), F32),
                   jax.ShapeDtypeStruct((b, DN_HEADS, DN_DK, DN_DV), F32)],
        scratch_shapes=[pltpu.VMEM((DN_HEADS, DN_DK, DN_DV), F32),
                        pltpu.VMEM((tm, DN_HEADS * DN_DV), F32)],
        compiler_params=_params(2),
        name="gdn_delta_prompt",
    )(q, k, kb, vb, gam, z, x, og, wout)


def _gdn_proj_sample_body(x_ref, cst_ref, gain_ref, win_ref, cw_ref, alog_ref, dtb_ref,
                          q_ref, k_ref, v_ref, eg_ref, beta_ref, z_ref, ncst_ref):
    qk = DN_HEADS * DN_DK
    cch = 3 * qk
    nb = x_ref.shape[0]
    x = x_ref[...]
    h = _rmsnorm(x, gain_ref[...]).astype(BF16)
    proj = _dot(h, win_ref[...])
    qkv = proj[:, :cch]
    z_ref[...] = proj[:, cch:cch + qk]
    ab = proj[:, cch + qk:cch + qk + LANES]
    cw = cw_ref[...]
    conv = qkv * cw[DN_CONV - 1:DN_CONV, :]
    for j in range(DN_CONV - 1):
        conv = conv + cst_ref[j] * cw[j:j + 1, :]
    c = _silu(conv)
    ncst_ref[:DN_CONV - 2] = cst_ref[1:]
    ncst_ref[DN_CONV - 2] = qkv
    g, beta = _gdn_gates(ab, alog_ref, dtb_ref)

    def emit(hh, qn, kn, vh, g_col, b_col):
        q_ref[:, hh, :] = qn
        k_ref[:, hh, :] = kn
        v_ref[:, hh, :] = vh
        eg_ref[:, hh, :] = jnp.broadcast_to(jnp.exp(g_col), (nb, DN_DK))
        beta_ref[:, hh, :] = jnp.broadcast_to(b_col, (nb, DN_DK))

    _gdn_heads(c, g, beta, emit)


def _gdn_proj_sample(x, cst, gain, win, cw, alog, dtb):
    nb, d = x.shape
    qk = DN_HEADS * DN_DK
    return pl.pallas_call(
        _gdn_proj_sample_body,
        out_shape=[jax.ShapeDtypeStruct((nb, DN_HEADS, DN_DK), F32)] * 5 + [
            jax.ShapeDtypeStruct((nb, qk), F32), jax.ShapeDtypeStruct(cst.shape, F32)],
        compiler_params=_params(0),
        name="gdn_proj_sample",
    )(x, cst, gain, win, cw, alog, dtb)


def _gdn_state_sample_body(q_ref, k_ref, v_ref, eg_ref, beta_ref, s_ref, o_ref, ns_ref, *, nblk):
    n_pair = DN_HEADS // 2
    row = lax.broadcasted_iota(jnp.int32, (DN_HEADS, n_pair * DN_DK), 0)
    col_pair = lax.broadcasted_iota(jnp.int32, (DN_HEADS, n_pair * DN_DK), 1) // DN_DK
    in_pair = (row // 2) == col_pair
    even = (lax.broadcasted_iota(jnp.int32, (DN_HEADS, DN_DV), 0) % 2) == 0

    def by_pair(a):
        return jnp.where(in_pair, jnp.concatenate([a] * n_pair, axis=1), 0.0).astype(BF16)

    def paired(s):
        return jnp.concatenate(
            [jnp.concatenate([s[2 * p], s[2 * p + 1]], axis=1) for p in range(n_pair)], axis=0)

    def own_half(a):
        return jnp.where(even, a[:, :DN_DV], a[:, DN_DV:])

    seqs = range(nblk)
    s_old = [[s_ref[r, hh] for hh in range(DN_HEADS)] for r in seqs]
    kp = [by_pair(k_ref[r]) for r in seqs]
    eg = [eg_ref[r] for r in seqs]
    kts = [own_half(_dot(kp[r], paired(s_old[r]).astype(BF16))) for r in seqs]
    w = [beta_ref[r] * (v_ref[r] - eg[r] * kts[r]) for r in seqs]
    w2 = [jnp.concatenate([jnp.where(even, x, 0.0), jnp.where(even, 0.0, x)], axis=1) for x in w]
    outer = [_dot_tn(kp[r], w2[r].astype(BF16)) for r in seqs]
    s_new = []
    for r in seqs:
        s_r = []
        for hh in range(DN_HEADS):
            p, b = divmod(hh, 2)
            blk = outer[r][p * DN_DK:(p + 1) * DN_DK, b * DN_DV:(b + 1) * DN_DV]
            s_r.append(s_old[r][hh] * eg[r][hh:hh + 1, :] + blk)
            ns_ref[r, hh] = s_r[hh]
        s_new.append(s_r)
    for r in seqs:
        o_ref[r] = own_half(_dot(by_pair(q_ref[r]), paired(s_new[r]).astype(BF16)))


def _gdn_state_sample(q, k, v, eg, beta, s, *, nblk):
    nb = s.shape[0]
    body = functools.partial(_gdn_state_sample_body, nblk=nblk)
    vec = pl.BlockSpec((nblk, DN_HEADS, DN_DK), lambda i: (i, 0, 0))
    st = pl.BlockSpec((nblk, DN_HEADS, DN_DK, DN_DV), lambda i: (i, 0, 0, 0))
    return pl.pallas_call(
        body,
        grid=(nb // nblk,),
        in_specs=[vec] * 5 + [st],
        out_specs=[vec, st],
        out_shape=[jax.ShapeDtypeStruct((nb, DN_HEADS, DN_DV), F32),
                   jax.ShapeDtypeStruct(s.shape, F32)],
        compiler_params=_params(1),
        name="gdn_state_sample",
    )(q, k, v, eg, beta, s)


def _gdn_out_sample_body(o_ref, z_ref, x_ref, og_ref, wout_ref, y_ref):
    o_heads = [o_ref[:, hh, :] for hh in range(DN_HEADS)]
    y_ref[...] = _gdn_out(o_heads, z_ref[...], x_ref[...], og_ref, wout_ref)


def _gdn_out_sample(o, z, x, og, wout):
    return pl.pallas_call(
        _gdn_out_sample_body,
        out_shape=jax.ShapeDtypeStruct(x.shape, F32),
        compiler_params=_params(0),
        name="gdn_out_sample",
    )(o, z, x, og, wout)


def _row(a):
    return a.reshape(1, -1).astype(F32)


def _time_major(st):
    return jnp.transpose(st, (1, 0, 2))


def _forward(x_prompt, x_sample, state_pool_l0, state_conv_l2, state_delta_l2, state_pool_l3,
             l0_norm, l0_pool_w_in, l0_pool_w_grp, l0_pool_scale, l0_pool_w_out,
             l1_norm, l1_sgu_w_in, l1_sgu_ln_g, l1_sgu_ln_b, l1_sgu_w_s, l1_sgu_b_s, l1_sgu_w_out,
             l2_norm, l2_dn_w_in, l2_dn_conv_w, l2_dn_a_log, l2_dn_dt_bias, l2_dn_o_gain, l2_dn_w_out,
             l3_norm, l3_pool_w_in, l3_pool_w_grp, l3_pool_scale, l3_pool_w_out,
             final_norm, *, tm_pool, tm_sgu, tm_gdn_proj, tm_gdn, nblk_state):
    b, t, d = x_prompt.shape
    nb = x_sample.shape[0]
    dp = l0_pool_scale.shape[0]
    ds = l1_sgu_ln_g.shape[0]
    gw_s = ds // SGU_GROUPS
    qk = DN_HEADS * DN_DK
    cch = 3 * qk
    fin = _row(final_norm)
    xs = x_sample

    def pool_layer(xp, xs_, st, norm, w_in, w_grp, scale, w_out, final):
        gain, sc = _row(norm), _row(scale)
        yp, st_p, win_b, wgrp_b, wout_b = _pool_prompt(
            xp, gain, sc, fin, w_in, w_grp.reshape(-1, w_grp.shape[-1]), w_out,
            tm=tm_pool, final_norm=final)
        ys, st_s = _pool_sample(xs_, _time_major(st), gain, win_b, wgrp_b, sc, wout_b, fin,
                                final_norm=final, y_shape=(nb, 1, d) if final else (nb, d))
        return yp, ys, st_p[:, POOL_HIST_PAD - POOL_HIST:, :], _time_major(st_s)

    xp, xs, pool0_p, pool0_s = pool_layer(
        x_prompt, xs, state_pool_l0, l0_norm, l0_pool_w_in, l0_pool_w_grp, l0_pool_scale,
        l0_pool_w_out, False)

    bsb = jnp.repeat(l1_sgu_b_s.T.astype(F32), gw_s, axis=1)
    xp, sgu_w_in, sgu_w_out = _sgu_prompt(
        xp.reshape(b * t, d), _row(l1_norm), _row(l1_sgu_ln_g), _row(l1_sgu_ln_b),
        l1_sgu_w_s.astype(F32), bsb, l1_sgu_w_in, l1_sgu_w_out, tm=tm_sgu)
    xp = xp.reshape(b, t, d)
    w00 = jnp.repeat(l1_sgu_w_s[:, 0, 0].astype(F32), gw_s)[None, :]
    b0 = jnp.repeat(l1_sgu_b_s[:, 0].astype(F32), gw_s)[None, :]
    xs, sgu1_s = _sgu_sample(xs, _row(l1_norm), sgu_w_in, _row(l1_sgu_ln_g), _row(l1_sgu_ln_b),
                             w00, b0, sgu_w_out)

    n_proj = l2_dn_w_in.shape[1]
    n_pad = cch + qk + LANES
    dn_w_in = jnp.pad(l2_dn_w_in, ((0, 0), (0, n_pad - n_proj))).astype(BF16)
    dn_w_out = l2_dn_w_out.astype(BF16)
    alog = jnp.pad(l2_dn_a_log.astype(F32), (0, LANES - DN_HEADS))[None, :]
    dtb = jnp.pad(l2_dn_dt_bias.astype(F32), (0, LANES - DN_HEADS))[None, :]
    cw = l2_dn_conv_w.astype(F32)
    og = _row(l2_dn_o_gain)
    q, k, kb, vb, gam, z, conv2_p = _gdn_proj_prompt(xp, _row(l2_norm), dn_w_in, cw, alog, dtb,
                                                     tm=tm_gdn_proj)
    xp, dn2_p = _gdn_delta_prompt(q, k, kb, vb, gam, z, xp, og, dn_w_out, tm=tm_gdn)
    conv2_p = conv2_p[:, CONV_HIST_PAD - (DN_CONV - 1):, :]

    qs, ks, vs, egs, betas, zs, conv2_s = _gdn_proj_sample(
        xs, _time_major(state_conv_l2), _row(l2_norm), dn_w_in, cw, alog, dtb)
    conv2_s = _time_major(conv2_s)
    o_s, dn2_s = _gdn_state_sample(qs, ks, vs, egs, betas, state_delta_l2, nblk=nblk_state)
    xs = _gdn_out_sample(o_s, zs, xs, og, dn_w_out)

    yp, ys, pool3_p, pool3_s = pool_layer(
        xp, xs, state_pool_l3, l3_norm, l3_pool_w_in, l3_pool_w_grp, l3_pool_scale,
        l3_pool_w_out, True)

    return (yp, ys, pool0_p, pool0_s, sgu1_s,
            conv2_p, conv2_s, dn2_p, dn2_s, pool3_p, pool3_s)


def kernel(x_prompt, x_sample, state_pool_l0, state_conv_l2, state_delta_l2, state_pool_l3, l0_norm, l0_pool_w_in, l0_pool_w_grp, l0_pool_scale, l0_pool_w_out, l1_norm, l1_sgu_w_in, l1_sgu_ln_g, l1_sgu_ln_b, l1_sgu_w_s, l1_sgu_b_s, l1_sgu_w_out, l2_norm, l2_dn_w_in, l2_dn_conv_w, l2_dn_a_log, l2_dn_dt_bias, l2_dn_o_gain, l2_dn_w_out, l3_norm, l3_pool_w_in, l3_pool_w_grp, l3_pool_scale, l3_pool_w_out, final_norm):
    return _forward(
        x_prompt, x_sample, state_pool_l0, state_conv_l2, state_delta_l2, state_pool_l3,
        l0_norm, l0_pool_w_in, l0_pool_w_grp, l0_pool_scale, l0_pool_w_out,
        l1_norm, l1_sgu_w_in, l1_sgu_ln_g, l1_sgu_ln_b, l1_sgu_w_s, l1_sgu_b_s, l1_sgu_w_out,
        l2_norm, l2_dn_w_in, l2_dn_conv_w, l2_dn_a_log, l2_dn_dt_bias, l2_dn_o_gain, l2_dn_w_out,
        l3_norm, l3_pool_w_in, l3_pool_w_grp, l3_pool_scale, l3_pool_w_out,
        final_norm, tm_pool=512, tm_sgu=512, tm_gdn_proj=512, tm_gdn=256, nblk_state=8)
```

```python
import functools

import jax
import jax.numpy as jnp
from jax import lax
from jax.experimental import pallas as pl
from jax.experimental.pallas import tpu as pltpu

F32 = jnp.float32
BF16 = jnp.bfloat16
EPS = 1e-6
PAST_LEN = 16384
POOL_WINDOWS = (2, 4, 8, 16)
POOL_HIST = 15
POOL_HIST_PAD = 16
SGU_CHUNK = 128
SGU_GROUPS = 4
DN_HEADS = 8
DN_DK = 128
DN_DV = 128
DN_CONV = 4
CONV_HIST_PAD = 8
DN_CHUNK = 128
INV_BASE_BITS = 4
LANES = 128
N_PREP = 8
VMEM_LIMIT = 60 * 1024 * 1024


def _rmsnorm(x, gain):
    ms = jnp.mean(x * x, axis=-1, keepdims=True)
    return x * lax.rsqrt(ms + EPS) * gain


def _silu(z):
    return z * jax.nn.sigmoid(z)


def _gelu(x):
    return 0.5 * x * (1.0 + lax.erf(x * (2.0 ** -0.5)))


def _dot(a, b):
    return jnp.dot(a, b, preferred_element_type=F32)


def _dot_nt(a, b):
    return lax.dot_general(a, b, (((1,), (1,)), ((), ())), preferred_element_type=F32)


def _dot_tn(a, b):
    return lax.dot_general(a, b, (((0,), (0,)), ((), ())), preferred_element_type=F32)


def _load_rows(ref):
    return ref[:, 0, :] if len(ref.shape) == 3 else ref[...]


def _store_rows(ref, val):
    if len(ref.shape) == 3:
        ref[:, 0, :] = val
    else:
        ref[...] = val


def _const_spec(shape):
    nd = len(shape)
    return pl.BlockSpec(shape, lambda *_: (0,) * nd, pipeline_mode=pl.Buffered(1))


def _params(n_grid):
    return pltpu.CompilerParams(
        dimension_semantics=("arbitrary",) * n_grid, vmem_limit_bytes=VMEM_LIMIT)


def _prep_specs(w):
    rows, cols = w.shape
    chunk = (rows // N_PREP, cols)
    index = lambda i: (jnp.minimum(i, N_PREP - 1), 0)
    return (pl.BlockSpec(chunk, index), pl.BlockSpec(chunk, index),
            jax.ShapeDtypeStruct(w.shape, BF16), pltpu.VMEM(w.shape, BF16))


def _prep_weights(step, f32_refs, out_refs, resident_refs):
    for f_ref, o_ref, r_ref in zip(f32_refs, out_refs, resident_refs):
        rows = f_ref.shape[0]
        chunk = f_ref[...].astype(BF16)
        o_ref[...] = chunk
        r_ref[pl.ds(pl.multiple_of(step * rows, rows), rows), :] = chunk


def _token_step(i, n_t):
    j = jnp.maximum(i - N_PREP, 0)
    return j // n_t, j % n_t


def _pool_tail(pooled_groups, z, x, wgrp_ref, scale_ref, wout_ref, fin_ref, final_norm):
    gw = pooled_groups[0].shape[1]
    mixed = [_dot(p.astype(BF16), wgrp_ref[gi * gw:(gi + 1) * gw, :]) for gi, p in enumerate(pooled_groups)]
    mixed = jnp.concatenate(mixed, axis=1)
    gated = mixed * scale_ref[...] * _silu(z)
    y = _dot(gated.astype(BF16), wout_ref[...]) + x
    if final_norm:
        y = _rmsnorm(y, fin_ref[...])
    return y


def _pool_prompt_body(x_ref, gain_ref, scale_ref, fin_ref, winf_ref, wgrpf_ref, woutf_ref,
                      y_ref, st_ref, wino_ref, wgrpo_ref, wouto_ref,
                      win_ref, wgrp_ref, wout_ref, hist_ref, *, tm, n_t, final_norm):
    i = pl.program_id(0)

    @pl.when(i < N_PREP)
    def _():
        _prep_weights(i, (winf_ref, wgrpf_ref, woutf_ref), (wino_ref, wgrpo_ref, wouto_ref),
                      (win_ref, wgrp_ref, wout_ref))

    @pl.when(i >= N_PREP)
    def _():
        _pool_prompt_tile(_token_step(i, n_t)[1], x_ref, gain_ref, scale_ref, fin_ref, y_ref, st_ref,
                          win_ref, wgrp_ref, wout_ref, hist_ref, tm=tm, n_t=n_t, final_norm=final_norm)


def _pool_prompt_tile(t, x_ref, gain_ref, scale_ref, fin_ref, y_ref, st_ref,
                      win_ref, wgrp_ref, wout_ref, hist_ref, *, tm, n_t, final_norm):
    dp = scale_ref.shape[-1]
    gw = dp // len(POOL_WINDOWS)

    @pl.when(t == 0)
    def _():
        hist_ref[...] = jnp.zeros_like(hist_ref)

    x = x_ref[...]
    h = _rmsnorm(x, gain_ref[...]).astype(BF16)
    xz = _dot(h, win_ref[...])
    xb = xz[:, :dp]
    z = xz[:, dp:]
    ext = jnp.concatenate([hist_ref[...], xb], axis=0)
    last = xb[tm - POOL_HIST_PAD:, :]
    hist_ref[...] = last
    pos1 = lax.broadcasted_iota(jnp.int32, (tm, LANES), 0) + (t * tm + 1)
    pooled = []
    for gi, w in enumerate(POOL_WINDOWS):
        e = ext[:, gi * gw:(gi + 1) * gw]
        s = e + pltpu.roll(e, 1, 0)
        sh = 2
        while sh < w:
            s = s + pltpu.roll(s, sh, 0)
            sh *= 2
        s = s[POOL_HIST_PAD:, :]
        inv = 1.0 / jnp.minimum(pos1, w).astype(F32)
        inv = jnp.concatenate([inv] * (gw // LANES), axis=1)
        pooled.append(s * inv - xb[:, gi * gw:(gi + 1) * gw])
    y_ref[...] = _pool_tail(pooled, z, x, wgrp_ref, scale_ref, wout_ref, fin_ref, final_norm)

    @pl.when(t == n_t - 1)
    def _():
        st_ref[...] = last


def _pool_prompt(x, gain, scale, fin, win, wgrp, wout, *, tm, final_norm):
    b, t, d = x.shape
    dp = scale.shape[-1]
    n_t = t // tm
    body = functools.partial(_pool_prompt_body, tm=tm, n_t=n_t, final_norm=final_norm)
    w_in_specs, w_out_specs, w_out_shapes, w_scratch = zip(*[_prep_specs(w) for w in (win, wgrp, wout)])
    tok = lambda i: _token_step(i, n_t) + (0,)
    return pl.pallas_call(
        body,
        grid=(N_PREP + b * n_t,),
        in_specs=[pl.BlockSpec((None, tm, d), tok), _const_spec(gain.shape), _const_spec(scale.shape),
                  _const_spec(fin.shape), *w_in_specs],
        out_specs=[pl.BlockSpec((None, tm, d), tok),
                   pl.BlockSpec((None, POOL_HIST_PAD, dp), lambda i: (_token_step(i, n_t)[0], 0, 0)),
                   *w_out_specs],
        out_shape=[jax.ShapeDtypeStruct((b, t, d), F32),
                   jax.ShapeDtypeStruct((b, POOL_HIST_PAD, dp), F32), *w_out_shapes],
        scratch_shapes=[*w_scratch, pltpu.VMEM((POOL_HIST_PAD, dp), F32)],
        compiler_params=_params(1),
        name="pool_prompt",
    )(x, gain, scale, fin, win, wgrp, wout)


def _pool_sample_body(x_ref, st_ref, gain_ref, win_ref, wgrp_ref, scale_ref, wout_ref, fin_ref,
                      y_ref, nst_ref, *, final_norm):
    dp = scale_ref.shape[-1]
    gw = dp // len(POOL_WINDOWS)
    x = _load_rows(x_ref)
    h = _rmsnorm(x, gain_ref[...]).astype(BF16)
    xz = _dot(h, win_ref[...])
    xb = xz[:, :dp]
    z = xz[:, dp:]
    pooled = []
    for gi, w in enumerate(POOL_WINDOWS):
        xg = xb[:, gi * gw:(gi + 1) * gw]
        s = xg
        for k in range(1, w):
            s = s + st_ref[POOL_HIST - k, :, gi * gw:(gi + 1) * gw]
        cnt = float(min(PAST_LEN + 1, w))
        pooled.append(s * (1.0 / cnt) - xg)
    _store_rows(y_ref, _pool_tail(pooled, z, x, wgrp_ref, scale_ref, wout_ref, fin_ref, final_norm))
    nst_ref[:POOL_HIST - 1] = st_ref[1:]
    nst_ref[POOL_HIST - 1] = xb


def _pool_sample(x, st, gain, win, wgrp, scale, wout, fin, *, final_norm, y_shape):
    body = functools.partial(_pool_sample_body, final_norm=final_norm)
    return pl.pallas_call(
        body,
        out_shape=[jax.ShapeDtypeStruct(y_shape, F32), jax.ShapeDtypeStruct(st.shape, F32)],
        compiler_params=_params(0),
        name="pool_sample",
    )(x, st, gain, win, wgrp, scale, wout, fin)


def _sgu_front(x, gain_ref, win_ref, lng_ref, lnb_ref):
    ds = lng_ref.shape[-1]
    h = _rmsnorm(x, gain_ref[...]).astype(BF16)
    uvz = _dot(h, win_ref[...])
    u = _gelu(uvz[:, :ds])
    v = _gelu(uvz[:, ds:2 * ds])
    z = _silu(uvz[:, 2 * ds:])
    mu = jnp.mean(v, axis=-1, keepdims=True)
    vc = v - mu
    var = jnp.mean(vc * vc, axis=-1, keepdims=True)
    vn = vc * lax.rsqrt(var + EPS) * lng_ref[...] + lnb_ref[...]
    return u, vn, z


def _sgu_prompt_body(x_ref, gain_ref, lng_ref, lnb_ref, ws_ref, bsb_ref, winf_ref, woutf_ref,
                     y_ref, wino_ref, wouto_ref, win_ref, wout_ref, *, tm):
    i = pl.program_id(0)

    @pl.when(i < N_PREP)
    def _():
        _prep_weights(i, (winf_ref, woutf_ref), (wino_ref, wouto_ref), (win_ref, wout_ref))

    @pl.when(i >= N_PREP)
    def _():
        _sgu_prompt_tile(x_ref, gain_ref, win_ref, lng_ref, lnb_ref, ws_ref, bsb_ref, wout_ref, y_ref, tm=tm)


def _sgu_prompt_tile(x_ref, gain_ref, win_ref, lng_ref, lnb_ref, ws_ref, bsb_ref, wout_ref,
                     y_ref, *, tm):
    ds = lng_ref.shape[-1]
    gw = ds // SGU_GROUPS
    x = x_ref[...]
    u, vn, z = _sgu_front(x, gain_ref, win_ref, lng_ref, lnb_ref)
    ri = lax.broadcasted_iota(jnp.int32, (SGU_CHUNK, SGU_CHUNK), 0)
    ci = lax.broadcasted_iota(jnp.int32, (SGU_CHUNK, SGU_CHUNK), 1)
    ws = [jnp.where(ri >= ci, ws_ref[g], 0.0).astype(BF16) for g in range(SGU_GROUPS)]
    vnb = vn.astype(BF16)
    bsb = bsb_ref[...]
    rows = []
    for n in range(tm // SGU_CHUNK):
        r0 = n * SGU_CHUNK
        cols = [_dot(ws[g], vnb[r0:r0 + SGU_CHUNK, g * gw:(g + 1) * gw]) for g in range(SGU_GROUPS)]
        rows.append(jnp.concatenate(cols, axis=1) + bsb)
    s = jnp.concatenate(rows, axis=0)
    gated = u * s * z
    y_ref[...] = _dot(gated.astype(BF16), wout_ref[...]) + x


def _sgu_prompt(x, gain, lng, lnb, ws, bsb, win, wout, *, tm):
    n, d = x.shape
    body = functools.partial(_sgu_prompt_body, tm=tm)
    w_in_specs, w_out_specs, w_out_shapes, w_scratch = zip(*[_prep_specs(w) for w in (win, wout)])
    tok = lambda i: (jnp.maximum(i - N_PREP, 0), 0)
    return pl.pallas_call(
        body,
        grid=(N_PREP + n // tm,),
        in_specs=[pl.BlockSpec((tm, d), tok)] + [
            _const_spec(a.shape) for a in (gain, lng, lnb, ws, bsb)] + [*w_in_specs],
        out_specs=[pl.BlockSpec((tm, d), tok), *w_out_specs],
        out_shape=[jax.ShapeDtypeStruct((n, d), F32), *w_out_shapes],
        scratch_shapes=[*w_scratch],
        compiler_params=_params(1),
        name="sgu_prompt",
    )(x, gain, lng, lnb, ws, bsb, win, wout)


def _sgu_sample_body(x_ref, gain_ref, win_ref, lng_ref, lnb_ref, w00_ref, b0_ref, wout_ref,
                     y_ref, v_ref):
    x = x_ref[...]
    u, vn, z = _sgu_front(x, gain_ref, win_ref, lng_ref, lnb_ref)
    s = vn * w00_ref[...] + b0_ref[...]
    gated = u * s * z
    y_ref[...] = _dot(gated.astype(BF16), wout_ref[...]) + x
    _store_rows(v_ref, vn)


def _sgu_sample(x, gain, win, lng, lnb, w00, b0, wout):
    nb, d = x.shape
    ds = lng.shape[-1]
    return pl.pallas_call(
        _sgu_sample_body,
        out_shape=[jax.ShapeDtypeStruct((nb, d), F32), jax.ShapeDtypeStruct((nb, 1, ds), F32)],
        compiler_params=_params(0),
        name="sgu_sample",
    )(x, gain, win, lng, lnb, w00, b0, wout)


def _gdn_gates(ab, alog_ref, dtb_ref):
    g = -jnp.exp(alog_ref[...]) * jax.nn.softplus(ab + dtb_ref[...])
    beta = jax.nn.sigmoid(ab)
    return g, beta


def _lane_pick(a, lane, idx):
    return jnp.sum(jnp.where(lane == idx, a, 0.0), axis=-1, keepdims=True)


def _gdn_heads(c, g_like, beta_all, emit):
    rows = c.shape[0]
    qk = DN_HEADS * DN_DK
    lane = lax.broadcasted_iota(jnp.int32, (rows, LANES), 1)
    for hh in range(DN_HEADS):
        qh = c[:, hh * DN_DK:(hh + 1) * DN_DK]
        kh = c[:, qk + hh * DN_DK:qk + (hh + 1) * DN_DK]
        vh = c[:, 2 * qk + hh * DN_DV:2 * qk + (hh + 1) * DN_DV]
        qn = qh * lax.rsqrt(jnp.sum(qh * qh, axis=-1, keepdims=True) + EPS) * (DN_DK ** -0.5)
        kn = kh * lax.rsqrt(jnp.sum(kh * kh, axis=-1, keepdims=True) + EPS)
        emit(hh, qn, kn, vh, _lane_pick(g_like, lane, hh), _lane_pick(beta_all, lane, DN_HEADS + hh))


def _gdn_proj_prompt_body(x_ref, gain_ref, win_ref, cw_ref, alog_ref, dtb_ref,
                          q_ref, k_ref, kb_ref, vb_ref, gam_ref, z_ref, cst_ref, hist_ref,
                          *, tm, n_t):
    t = pl.program_id(1)
    qk = DN_HEADS * DN_DK
    cch = 3 * qk

    @pl.when(t == 0)
    def _():
        hist_ref[...] = jnp.zeros_like(hist_ref)

    x = x_ref[...]
    h = _rmsnorm(x, gain_ref[...]).astype(BF16)
    proj = _dot(h, win_ref[...])
    qkv = proj[:, :cch]
    z_ref[...] = proj[:, cch:cch + qk]
    ab = proj[:, cch + qk:cch + qk + LANES]
    ext = jnp.concatenate([hist_ref[...], qkv], axis=0)
    last = qkv[tm - CONV_HIST_PAD:, :]
    hist_ref[...] = last
    cw = cw_ref[...]
    conv = ext * cw[DN_CONV - 1:DN_CONV, :]
    for j in range(1, DN_CONV):
        conv = conv + pltpu.roll(ext, j, 0) * cw[DN_CONV - 1 - j:DN_CONV - j, :]
    c = _silu(conv[CONV_HIST_PAD:, :])

    g, beta = _gdn_gates(ab, alog_ref, dtb_ref)
    rin = lax.broadcasted_iota(jnp.int32, (tm, LANES), 0) & (DN_CHUNK - 1)
    gam = g
    sh = 1
    while sh < DN_CHUNK:
        gam = gam + jnp.where(rin >= sh, pltpu.roll(gam, sh, 0), 0.0)
        sh *= 2

    def emit(hh, qn, kn, vh, g_col, b_col):
        sl = slice(hh * DN_DK, (hh + 1) * DN_DK)
        q_ref[:, sl] = qn.astype(BF16)
        k_ref[:, sl] = kn.astype(BF16)
        kb_ref[:, sl] = (b_col * kn).astype(BF16)
        vb_ref[:, sl] = (b_col * vh).astype(BF16)
        gam_ref[:, sl] = jnp.broadcast_to(g_col, (tm, DN_DK))

    _gdn_heads(c, gam, beta, emit)

    @pl.when(t == n_t - 1)
    def _():
        cst_ref[...] = last


def _gdn_proj_prompt(x, gain, win, cw, alog, dtb, *, tm):
    b, t, d = x.shape
    qk = DN_HEADS * DN_DK
    cch = 3 * qk
    n_t = t // tm
    body = functools.partial(_gdn_proj_prompt_body, tm=tm, n_t=n_t)
    tok = lambda i, j: (i, j, 0)
    return pl.pallas_call(
        body,
        grid=(b, n_t),
        in_specs=[pl.BlockSpec((None, tm, d), tok)] + [
            _const_spec(a.shape) for a in (gain, win, cw, alog, dtb)],
        out_specs=[pl.BlockSpec((None, tm, qk), tok)] * 6 + [
            pl.BlockSpec((None, CONV_HIST_PAD, cch), lambda i, j: (i, 0, 0))],
        out_shape=[jax.ShapeDtypeStruct((b, t, qk), BF16)] * 4 + [
            jax.ShapeDtypeStruct((b, t, qk), F32)] * 2 + [
            jax.ShapeDtypeStruct((b, CONV_HIST_PAD, cch), F32)],
        scratch_shapes=[pltpu.VMEM((CONV_HIST_PAD, cch), F32)],
        compiler_params=_params(2),
        name="gdn_proj_prompt",
    )(x, gain, win, cw, alog, dtb)


def _unit_lower_inverse_minus_eye(ms, in_base, at_level):
    bf = lambda xs: [x.astype(BF16) for x in xs]
    a = [jnp.where(in_base, m, 0.0) for m in ms]
    e = [-x for x in a]
    ab = bf(a)
    p = [_dot(x, x) for x in ab]
    n = ms[0].shape[0]
    for it in range(INV_BASE_BITS - 1):
        pb = bf(p)
        if it < INV_BASE_BITS - 2:
            both = [_dot(jnp.concatenate([x, y], axis=0), y) for x, y in zip(bf(e), pb)]
            ep = [x[:n] for x in both]
            p_next = [x[n:] for x in both]
        else:
            ep = [_dot(x, y) for x, y in zip(bf(e), pb)]
        e = [x + y + z for x, y, z in zip(e, p, ep)]
        if it < INV_BASE_BITS - 2:
            p = p_next
    for mask in at_level:
        low = [jnp.where(mask, m, 0.0) for m in ms]
        eb = bf(e)
        x = [lo + _dot(lb, y) for lo, lb, y in zip(low, bf(low), eb)]
        ex = [_dot(y, xb) for y, xb in zip(eb, bf(x))]
        e = [ei - (xi + yi) for ei, xi, yi in zip(e, x, ex)]
    return e


def _gdn_out(o_heads, z, x, og_ref, wout_ref):
    normed = [oh * lax.rsqrt(jnp.mean(oh * oh, axis=-1, keepdims=True) + EPS) * og_ref[...]
              for oh in o_heads]
    gated = jnp.concatenate(normed, axis=1) * _silu(z)
    return _dot(gated.astype(BF16), wout_ref[...]) + x


def _gdn_delta_prompt_body(q_ref, k_ref, kb_ref, vb_ref, gam_ref, z_ref, x_ref, og_ref, wout_ref,
                           y_ref, sfin_ref, s_ref, o_ref, *, tm, n_t):
    t = pl.program_id(1)

    @pl.when(t == 0)
    def _():
        s_ref[...] = jnp.zeros_like(s_ref)

    c = DN_CHUNK
    ri = lax.broadcasted_iota(jnp.int32, (c, c), 0)
    ci = lax.broadcasted_iota(jnp.int32, (c, c), 1)
    incl = ri >= ci
    strict = ri > ci
    top = ri ^ ci
    in_base = (top >> INV_BASE_BITS) == 0
    levels = range(INV_BASE_BITS, c.bit_length() - 1)
    at_level = [(top >> lv) == 1 for lv in levels]
    n_chunks = tm // c
    pairs = [(n, hh) for n in range(n_chunks) for hh in range(DN_HEADS)]

    def blk(ref, n, hh):
        return ref[n * c:(n + 1) * c, hh * DN_DK:(hh + 1) * DN_DK]

    kh = [blk(k_ref, *p) for p in pairs]
    kbh = [blk(kb_ref, *p) for p in pairs]
    gam = [blk(gam_ref, *p) for p in pairs]
    qh = [blk(q_ref, *p) for p in pairs]
    kq = [_dot_nt(jnp.concatenate([a, q], axis=0), b) for a, q, b in zip(kbh, qh, kh)]
    kk = [x[:c] for x in kq]
    dec = [jnp.exp(jnp.where(incl, g - g.T, -1e30)) for g in gam]
    m = [jnp.where(strict, a * d, 0.0) for a, d in zip(kk, dec)]
    e = _unit_lower_inverse_minus_eye(m, in_base, at_level)
    eg = [jnp.exp(g) for g in gam]
    rhs = [jnp.concatenate([blk(vb_ref, *p).astype(F32), a.astype(F32) * x], axis=1)
           for p, a, x in zip(pairs, kbh, eg)]
    uw = [r + _dot(x.astype(BF16), r.astype(BF16)) for r, x in zip(rhs, e)]
    qkm = [(x[c:] * d).astype(BF16) for x, d in zip(kq, dec)]
    qd = [(a.astype(F32) * x).astype(BF16) for a, x in zip(qh, eg)]
    gl = [g[c - 1:c, :] for g in gam]
    kt = [(a.astype(F32) * jnp.exp(l - g)).astype(BF16) for a, l, g in zip(kh, gl, gam)]
    al = [jnp.exp(l) for l in gl]

    for n in range(n_chunks):
        idx = [n * DN_HEADS + hh for hh in range(DN_HEADS)]
        s_old = [s_ref[hh] for hh in range(DN_HEADS)]
        sb = [x.astype(BF16) for x in s_old]
        ws = [_dot(jnp.concatenate([uw[i][:, DN_DV:].astype(BF16), qd[i]], axis=0), y)
              for i, y in zip(idx, sb)]
        w = [uw[i][:, :DN_DV] - x[:c] for i, x in zip(idx, ws)]
        wb = [x.astype(BF16) for x in w]
        for hh, i in enumerate(idx):
            o_ref[n * c:(n + 1) * c, hh * DN_DV:(hh + 1) * DN_DV] = ws[hh][c:] + _dot(qkm[i], wb[hh])
        for hh, i in enumerate(idx):
            s_ref[hh] = al[i] * s_old[hh] + _dot_tn(kt[i], wb[hh])
    o = o_ref[...]
    o_heads = [o[:, hh * DN_DV:(hh + 1) * DN_DV] for hh in range(DN_HEADS)]
    y_ref[...] = _gdn_out(o_heads, z_ref[...], x_ref[...], og_ref, wout_ref)

    @pl.when(t == n_t - 1)
    def _():
        sfin_ref[...] = s_ref[...]


def _gdn_delta_prompt(q, k, kb, vb, gam, z, x, og, wout, *, tm):
    b, t, d = x.shape
    qk = DN_HEADS * DN_DK
    n_t = t // tm
    body = functools.partial(_gdn_delta_prompt_body, tm=tm, n_t=n_t)
    tok = lambda i, j: (i, j, 0)
    return pl.pallas_call(
        body,
        grid=(b, n_t),
        in_specs=[pl.BlockSpec((None, tm, qk), tok)] * 6 + [pl.BlockSpec((None, tm, d), tok),
                  _const_spec(og.shape), _const_spec(wout.shape)],
        out_specs=[pl.BlockSpec((None, tm, d), tok),
                   pl.BlockSpec((None, DN_HEADS, DN_DK, DN_DV), lambda i, j: (i, 0, 0, 0))],
        out_shape=[jax.ShapeDtypeStruct((b, t, d), F32),
                   jax.ShapeDtypeStruct((b, DN_HEADS, DN_DK, DN_DV), F32)],
        scratch_shapes=[pltpu.VMEM((DN_HEADS, DN_DK, DN_DV), F32),
                        pltpu.VMEM((tm, DN_HEADS * DN_DV), F32)],
        compiler_params=_params(2),
        name="gdn_delta_prompt",
    )(q, k, kb, vb, gam, z, x, og, wout)


def _gdn_proj_sample_body(x_ref, cst_ref, gain_ref, win_ref, cw_ref, alog_ref, dtb_ref,
                          q_ref, k_ref, v_ref, eg_ref, beta_ref, z_ref, ncst_ref):
    qk = DN_HEADS * DN_DK
    cch = 3 * qk
    nb = x_ref.shape[0]
    x = x_ref[...]
    h = _rmsnorm(x, gain_ref[...]).astype(BF16)
    proj = _dot(h, win_ref[...])
    qkv = proj[:, :cch]
    z_ref[...] = proj[:, cch:cch + qk]
    ab = proj[:, cch + qk:cch + qk + LANES]
    cw = cw_ref[...]
    conv = qkv * cw[DN_CONV - 1:DN_CONV, :]
    for j in range(DN_CONV - 1):
        conv = conv + cst_ref[j] * cw[j:j + 1, :]
    c = _silu(conv)
    ncst_ref[:DN_CONV - 2] = cst_ref[1:]
    ncst_ref[DN_CONV - 2] = qkv
    g, beta = _gdn_gates(ab, alog_ref, dtb_ref)

    def emit(hh, qn, kn, vh, g_col, b_col):
        q_ref[:, hh, :] = qn
        k_ref[:, hh, :] = kn
        v_ref[:, hh, :] = vh
        eg_ref[:, hh, :] = jnp.broadcast_to(jnp.exp(g_col), (nb, DN_DK))
        beta_ref[:, hh, :] = jnp.broadcast_to(b_col, (nb, DN_DK))

    _gdn_heads(c, g, beta, emit)


def _gdn_proj_sample(x, cst, gain, win, cw, alog, dtb):
    nb, d = x.shape
    qk = DN_HEADS * DN_DK
    return pl.pallas_call(
        _gdn_proj_sample_body,
        out_shape=[jax.ShapeDtypeStruct((nb, DN_HEADS, DN_DK), F32)] * 5 + [
            jax.ShapeDtypeStruct((nb, qk), F32), jax.ShapeDtypeStruct(cst.shape, F32)],
        compiler_params=_params(0),
        name="gdn_proj_sample",
    )(x, cst, gain, win, cw, alog, dtb)


def _gdn_state_sample_body(q_ref, k_ref, v_ref, eg_ref, beta_ref, s_ref, o_ref, ns_ref, *, nblk):
    n_pair = DN_HEADS // 2
    row = lax.broadcasted_iota(jnp.int32, (DN_HEADS, n_pair * DN_DK), 0)
    col_pair = lax.broadcasted_iota(jnp.int32, (DN_HEADS, n_pair * DN_DK), 1) // DN_DK
    in_pair = (row // 2) == col_pair
    even = (lax.broadcasted_iota(jnp.int32, (DN_HEADS, DN_DV), 0) % 2) == 0

    def by_pair(a):
        return jnp.where(in_pair, jnp.concatenate([a] * n_pair, axis=1), 0.0).astype(BF16)

    def paired(s):
        return jnp.concatenate(
            [jnp.concatenate([s[2 * p], s[2 * p + 1]], axis=1) for p in range(n_pair)], axis=0)

    def own_half(a):
        return jnp.where(even, a[:, :DN_DV], a[:, DN_DV:])

    seqs = range(nblk)
    s_old = [[s_ref[r, hh] for hh in range(DN_HEADS)] for r in seqs]
    kp = [by_pair(k_ref[r]) for r in seqs]
    eg = [eg_ref[r] for r in seqs]
    kts = [own_half(_dot(kp[r], paired(s_old[r]).astype(BF16))) for r in seqs]
    w = [beta_ref[r] * (v_ref[r] - eg[r] * kts[r]) for r in seqs]
    w2 = [jnp.concatenate([jnp.where(even, x, 0.0), jnp.where(even, 0.0, x)], axis=1) for x in w]
    outer = [_dot_tn(kp[r], w2[r].astype(BF16)) for r in seqs]
    s_new = []
    for r in seqs:
        s_r = []
        for hh in range(DN_HEADS):
            p, b = divmod(hh, 2)
            blk = outer[r][p * DN_DK:(p + 1) * DN_DK, b * DN_DV:(b + 1) * DN_DV]
            s_r.append(s_old[r][hh] * eg[r][hh:hh + 1, :] + blk)
            ns_ref[r, hh] = s_r[hh]
        s_new.append(s_r)
    for r in seqs:
        o_ref[r] = own_half(_dot(by_pair(q_ref[r]), paired(s_new[r]).astype(BF16)))


def _gdn_state_sample(q, k, v, eg, beta, s, *, nblk):
    nb = s.shape[0]
    body = functools.partial(_gdn_state_sample_body, nblk=nblk)
    vec = pl.BlockSpec((nblk, DN_HEADS, DN_DK), lambda i: (i, 0, 0))
    st = pl.BlockSpec((nblk, DN_HEADS, DN_DK, DN_DV), lambda i: (i, 0, 0, 0))
    return pl.pallas_call(
        body,
        grid=(nb // nblk,),
        in_specs=[vec] * 5 + [st],
        out_specs=[vec, st],
        out_shape=[jax.ShapeDtypeStruct((nb, DN_HEADS, DN_DV), F32),
                   jax.ShapeDtypeStruct(s.shape, F32)],
        compiler_params=_params(1),
        name="gdn_state_sample",
    )(q, k, v, eg, beta, s)


def _gdn_out_sample_body(o_ref, z_ref, x_ref, og_ref, wout_ref, y_ref):
    o_heads = [o_ref[:, hh, :] for hh in range(DN_HEADS)]
    y_ref[...] = _gdn_out(o_heads, z_ref[...], x_ref[...], og_ref, wout_ref)


def _gdn_out_sample(o, z, x, og, wout):
    return pl.pallas_call(
        _gdn_out_sample_body,
        out_shape=jax.ShapeDtypeStruct(x.shape, F32),
        compiler_params=_params(0),
        name="gdn_out_sample",
    )(o, z, x, og, wout)


def _row(a):
    return a.reshape(1, -1).astype(F32)


def _time_major(st):
    return jnp.transpose(st, (1, 0, 2))


def _forward(x_prompt, x_sample, state_pool_l0, state_conv_l2, state_delta_l2, state_pool_l3,
             l0_norm, l0_pool_w_in, l0_pool_w_grp, l0_pool_scale, l0_pool_w_out,
             l1_norm, l1_sgu_w_in, l1_sgu_ln_g, l1_sgu_ln_b, l1_sgu_w_s, l1_sgu_b_s, l1_sgu_w_out,
             l2_norm, l2_dn_w_in, l2_dn_conv_w, l2_dn_a_log, l2_dn_dt_bias, l2_dn_o_gain, l2_dn_w_out,
             l3_norm, l3_pool_w_in, l3_pool_w_grp, l3_pool_scale, l3_pool_w_out,
             final_norm, *, tm_pool, tm_sgu, tm_gdn_proj, tm_gdn, nblk_state):
    b, t, d = x_prompt.shape
    nb = x_sample.shape[0]
    dp = l0_pool_scale.shape[0]
    ds = l1_sgu_ln_g.shape[0]
    gw_s = ds // SGU_GROUPS
    qk = DN_HEADS * DN_DK
    cch = 3 * qk
    fin = _row(final_norm)
    xs = x_sample

    def pool_layer(xp, xs_, st, norm, w_in, w_grp, scale, w_out, final):
        gain, sc = _row(norm), _row(scale)
        yp, st_p, win_b, wgrp_b, wout_b = _pool_prompt(
            xp, gain, sc, fin, w_in, w_grp.reshape(-1, w_grp.shape[-1]), w_out,
            tm=tm_pool, final_norm=final)
        ys, st_s = _pool_sample(xs_, _time_major(st), gain, win_b, wgrp_b, sc, wout_b, fin,
                                final_norm=final, y_shape=(nb, 1, d) if final else (nb, d))
        return yp, ys, st_p[:, POOL_HIST_PAD - POOL_HIST:, :], _time_major(st_s)

    xp, xs, pool0_p, pool0_s = pool_layer(
        x_prompt, xs, state_pool_l0, l0_norm, l0_pool_w_in, l0_pool_w_grp, l0_pool_scale,
        l0_pool_w_out, False)

    bsb = jnp.repeat(l1_sgu_b_s.T.astype(F32), gw_s, axis=1)
    xp, sgu_w_in, sgu_w_out = _sgu_prompt(
        xp.reshape(b * t, d), _row(l1_norm), _row(l1_sgu_ln_g), _row(l1_sgu_ln_b),
        l1_sgu_w_s.astype(F32), bsb, l1_sgu_w_in, l1_sgu_w_out, tm=tm_sgu)
    xp = xp.reshape(b, t, d)
    w00 = jnp.repeat(l1_sgu_w_s[:, 0, 0].astype(F32), gw_s)[None, :]
    b0 = jnp.repeat(l1_sgu_b_s[:, 0].astype(F32), gw_s)[None, :]
    xs, sgu1_s = _sgu_sample(xs, _row(l1_norm), sgu_w_in, _row(l1_sgu_ln_g), _row(l1_sgu_ln_b),
                             w00, b0, sgu_w_out)

    n_proj = l2_dn_w_in.shape[1]
    n_pad = cch + qk + LANES
    dn_w_in = jnp.pad(l2_dn_w_in, ((0, 0), (0, n_pad - n_proj))).astype(BF16)
    dn_w_out = l2_dn_w_out.astype(BF16)
    alog = jnp.pad(l2_dn_a_log.astype(F32), (0, LANES - DN_HEADS))[None, :]
    dtb = jnp.pad(l2_dn_dt_bias.astype(F32), (0, LANES - DN_HEADS))[None, :]
    cw = l2_dn_conv_w.astype(F32)
    og = _row(l2_dn_o_gain)
    q, k, kb, vb, gam, z, conv2_p = _gdn_proj_prompt(xp, _row(l2_norm), dn_w_in, cw, alog, dtb,
                                                     tm=tm_gdn_proj)
    xp, dn2_p = _gdn_delta_prompt(q, k, kb, vb, gam, z, xp, og, dn_w_out, tm=tm_gdn)
    conv2_p = conv2_p[:, CONV_HIST_PAD - (DN_CONV - 1):, :]

    qs, ks, vs, egs, betas, zs, conv2_s = _gdn_proj_sample(
        xs, _time_major(state_conv_l2), _row(l2_norm), dn_w_in, cw, alog, dtb)
    conv2_s = _time_major(conv2_s)
    o_s, dn2_s = _gdn_state_sample(qs, ks, vs, egs, betas, state_delta_l2, nblk=nblk_state)
    xs = _gdn_out_sample(o_s, zs, xs, og, dn_w_out)

    yp, ys, pool3_p, pool3_s = pool_layer(
        xp, xs, state_pool_l3, l3_norm, l3_pool_w_in, l3_pool_w_grp, l3_pool_scale,
        l3_pool_w_out, True)

    return (yp, ys, pool0_p, pool0_s, sgu1_s,
            conv2_p, conv2_s, dn2_p, dn2_s, pool3_p, pool3_s)


def kernel(x_prompt, x_sample, state_pool_l0, state_conv_l2, state_delta_l2, state_pool_l3, l0_norm, l0_pool_w_in, l0_pool_w_grp, l0_pool_scale, l0_pool_w_out, l1_norm, l1_sgu_w_in, l1_sgu_ln_g, l1_sgu_ln_b, l1_sgu_w_s, l1_sgu_b_s, l1_sgu_w_out, l2_norm, l2_dn_w_in, l2_dn_conv_w, l2_dn_a_log, l2_dn_dt_bias, l2_dn_o_gain, l2_dn_w_out, l3_norm, l3_pool_w_in, l3_pool_w_grp, l3_pool_scale, l3_pool_w_out, final_norm):
    return _forward(
        x_prompt, x_sample, state_pool_l0, state_conv_l2, state_delta_l2, state_pool_l3,
        l0_norm, l0_pool_w_in, l0_pool_w_grp, l0_pool_scale, l0_pool_w_out,
        l1_norm, l1_sgu_w_in, l1_sgu_ln_g, l1_sgu_ln_b, l1_sgu_w_s, l1_sgu_b_s, l1_sgu_w_out,
        l2_norm, l2_dn_w_in, l2_dn_conv_w, l2_dn_a_log, l2_dn_dt_bias, l2_dn_o_gain, l2_dn_w_out,
        l3_norm, l3_pool_w_in, l3_pool_w_grp, l3_pool_scale, l3_pool_w_out,
        final_norm, tm_pool=1024, tm_sgu=512, tm_gdn_proj=512, tm_gdn=256, nblk_state=8)
```

```python
import functools

import jax
import jax.numpy as jnp
from jax import lax
from jax.experimental import pallas as pl
from jax.experimental.pallas import tpu as pltpu

F32 = jnp.float32
BF16 = jnp.bfloat16
EPS = 1e-6
PAST_LEN = 16384
POOL_WINDOWS = (2, 4, 8, 16)
POOL_HIST = 15
POOL_HIST_PAD = 16
SGU_CHUNK = 128
SGU_GROUPS = 4
DN_HEADS = 8
DN_DK = 128
DN_DV = 128
DN_CONV = 4
CONV_HIST_PAD = 8
DN_CHUNK = 128
INV_BASE_BITS = 4
LANES = 128
N_PREP = 8
VMEM_LIMIT = 60 * 1024 * 1024


def _rmsnorm(x, gain):
    ms = jnp.mean(x * x, axis=-1, keepdims=True)
    return x * lax.rsqrt(ms + EPS) * gain


def _silu(z):
    h = 0.5 * z
    return h + h * jnp.tanh(h)


def _gelu(x):
    return 0.5 * x * (1.0 + lax.erf(x * (2.0 ** -0.5)))


def _dot(a, b):
    return jnp.dot(a, b, preferred_element_type=F32)


def _dot_nt(a, b):
    return lax.dot_general(a, b, (((1,), (1,)), ((), ())), preferred_element_type=F32)


def _dot_tn(a, b):
    return lax.dot_general(a, b, (((0,), (0,)), ((), ())), preferred_element_type=F32)


def _load_rows(ref):
    return ref[:, 0, :] if len(ref.shape) == 3 else ref[...]


def _store_rows(ref, val):
    if len(ref.shape) == 3:
        ref[:, 0, :] = val
    else:
        ref[...] = val


def _const_spec(shape):
    nd = len(shape)
    return pl.BlockSpec(shape, lambda *_: (0,) * nd, pipeline_mode=pl.Buffered(1))


def _params(n_grid):
    return pltpu.CompilerParams(
        dimension_semantics=("arbitrary",) * n_grid, vmem_limit_bytes=VMEM_LIMIT)


def _prep_specs(w):
    rows, cols = w.shape
    chunk = (rows // N_PREP, cols)
    index = lambda i: (jnp.minimum(i, N_PREP - 1), 0)
    return (pl.BlockSpec(chunk, index), pl.BlockSpec(chunk, index),
            jax.ShapeDtypeStruct(w.shape, BF16), pltpu.VMEM(w.shape, BF16))


def _prep_weights(step, f32_refs, out_refs, resident_refs):
    for f_ref, o_ref, r_ref in zip(f32_refs, out_refs, resident_refs):
        rows = f_ref.shape[0]
        chunk = f_ref[...].astype(BF16)
        o_ref[...] = chunk
        r_ref[pl.ds(pl.multiple_of(step * rows, rows), rows), :] = chunk


def _token_step(i, n_t):
    j = jnp.maximum(i - N_PREP, 0)
    return j // n_t, j % n_t


def _pool_tail(pooled_groups, z, x, wgrp_ref, scale_ref, wout_ref, fin_ref, final_norm):
    gw = pooled_groups[0].shape[1]
    mixed = [_dot(p.astype(BF16), wgrp_ref[gi * gw:(gi + 1) * gw, :]) for gi, p in enumerate(pooled_groups)]
    mixed = jnp.concatenate(mixed, axis=1)
    gated = mixed * scale_ref[...] * _silu(z)
    y = _dot(gated.astype(BF16), wout_ref[...]) + x
    if final_norm:
        y = _rmsnorm(y, fin_ref[...])
    return y


def _pool_prompt_body(x_ref, gain_ref, scale_ref, fin_ref, winf_ref, wgrpf_ref, woutf_ref,
                      y_ref, st_ref, wino_ref, wgrpo_ref, wouto_ref,
                      win_ref, wgrp_ref, wout_ref, hist_ref, *, tm, n_t, final_norm):
    i = pl.program_id(0)

    @pl.when(i < N_PREP)
    def _():
        _prep_weights(i, (winf_ref, wgrpf_ref, woutf_ref), (wino_ref, wgrpo_ref, wouto_ref),
                      (win_ref, wgrp_ref, wout_ref))

    @pl.when(i >= N_PREP)
    def _():
        _pool_prompt_tile(_token_step(i, n_t)[1], x_ref, gain_ref, scale_ref, fin_ref, y_ref, st_ref,
                          win_ref, wgrp_ref, wout_ref, hist_ref, tm=tm, n_t=n_t, final_norm=final_norm)


def _pool_prompt_tile(t, x_ref, gain_ref, scale_ref, fin_ref, y_ref, st_ref,
                      win_ref, wgrp_ref, wout_ref, hist_ref, *, tm, n_t, final_norm):
    dp = scale_ref.shape[-1]
    gw = dp // len(POOL_WINDOWS)

    @pl.when(t == 0)
    def _():
        hist_ref[...] = jnp.zeros_like(hist_ref)

    x = x_ref[...]
    h = _rmsnorm(x, gain_ref[...]).astype(BF16)
    xz = _dot(h, win_ref[...])
    xb = xz[:, :dp]
    z = xz[:, dp:]
    ext = jnp.concatenate([hist_ref[...], xb], axis=0)
    last = xb[tm - POOL_HIST_PAD:, :]
    hist_ref[...] = last
    pos1 = lax.broadcasted_iota(jnp.int32, (tm, LANES), 0) + (t * tm + 1)
    pooled = []
    for gi, w in enumerate(POOL_WINDOWS):
        e = ext[:, gi * gw:(gi + 1) * gw]
        s = e + pltpu.roll(e, 1, 0)
        sh = 2
        while sh < w:
            s = s + pltpu.roll(s, sh, 0)
            sh *= 2
        s = s[POOL_HIST_PAD:, :]
        inv = 1.0 / jnp.minimum(pos1, w).astype(F32)
        inv = jnp.concatenate([inv] * (gw // LANES), axis=1)
        pooled.append(s * inv - xb[:, gi * gw:(gi + 1) * gw])
    y_ref[...] = _pool_tail(pooled, z, x, wgrp_ref, scale_ref, wout_ref, fin_ref, final_norm)

    @pl.when(t == n_t - 1)
    def _():
        st_ref[...] = last


def _pool_prompt(x, gain, scale, fin, win, wgrp, wout, *, tm, final_norm):
    b, t, d = x.shape
    dp = scale.shape[-1]
    n_t = t // tm
    body = functools.partial(_pool_prompt_body, tm=tm, n_t=n_t, final_norm=final_norm)
    w_in_specs, w_out_specs, w_out_shapes, w_scratch = zip(*[_prep_specs(w) for w in (win, wgrp, wout)])
    tok = lambda i: _token_step(i, n_t) + (0,)
    return pl.pallas_call(
        body,
        grid=(N_PREP + b * n_t,),
        in_specs=[pl.BlockSpec((None, tm, d), tok), _const_spec(gain.shape), _const_spec(scale.shape),
                  _const_spec(fin.shape), *w_in_specs],
        out_specs=[pl.BlockSpec((None, tm, d), tok),
                   pl.BlockSpec((None, POOL_HIST_PAD, dp), lambda i: (_token_step(i, n_t)[0], 0, 0)),
                   *w_out_specs],
        out_shape=[jax.ShapeDtypeStruct((b, t, d), F32),
                   jax.ShapeDtypeStruct((b, POOL_HIST_PAD, dp), F32), *w_out_shapes],
        scratch_shapes=[*w_scratch, pltpu.VMEM((POOL_HIST_PAD, dp), F32)],
        compiler_params=_params(1),
        name="pool_prompt",
    )(x, gain, scale, fin, win, wgrp, wout)


def _pool_sample_body(x_ref, st_ref, gain_ref, win_ref, wgrp_ref, scale_ref, wout_ref, fin_ref,
                      y_ref, nst_ref, *, final_norm):
    dp = scale_ref.shape[-1]
    gw = dp // len(POOL_WINDOWS)
    x = _load_rows(x_ref)
    h = _rmsnorm(x, gain_ref[...]).astype(BF16)
    xz = _dot(h, win_ref[...])
    xb = xz[:, :dp]
    z = xz[:, dp:]
    pooled = []
    for gi, w in enumerate(POOL_WINDOWS):
        xg = xb[:, gi * gw:(gi + 1) * gw]
        s = xg
        for k in range(1, w):
            s = s + st_ref[POOL_HIST - k, :, gi * gw:(gi + 1) * gw]
        cnt = float(min(PAST_LEN + 1, w))
        pooled.append(s * (1.0 / cnt) - xg)
    _store_rows(y_ref, _pool_tail(pooled, z, x, wgrp_ref, scale_ref, wout_ref, fin_ref, final_norm))
    nst_ref[:POOL_HIST - 1] = st_ref[1:]
    nst_ref[POOL_HIST - 1] = xb


def _pool_sample(x, st, gain, win, wgrp, scale, wout, fin, *, final_norm, y_shape):
    body = functools.partial(_pool_sample_body, final_norm=final_norm)
    return pl.pallas_call(
        body,
        out_shape=[jax.ShapeDtypeStruct(y_shape, F32), jax.ShapeDtypeStruct(st.shape, F32)],
        compiler_params=_params(0),
        name="pool_sample",
    )(x, st, gain, win, wgrp, scale, wout, fin)


def _sgu_front(x, gain_ref, win_ref, lng_ref, lnb_ref):
    ds = lng_ref.shape[-1]
    h = _rmsnorm(x, gain_ref[...]).astype(BF16)
    uvz = _dot(h, win_ref[...])
    u = _gelu(uvz[:, :ds])
    v = _gelu(uvz[:, ds:2 * ds])
    z = _silu(uvz[:, 2 * ds:])
    mu = jnp.mean(v, axis=-1, keepdims=True)
    vc = v - mu
    var = jnp.mean(vc * vc, axis=-1, keepdims=True)
    vn = vc * lax.rsqrt(var + EPS) * lng_ref[...] + lnb_ref[...]
    return u, vn, z


def _sgu_prompt_body(x_ref, gain_ref, lng_ref, lnb_ref, ws_ref, bsb_ref, winf_ref, woutf_ref,
                     y_ref, wino_ref, wouto_ref, win_ref, wout_ref, *, tm):
    i = pl.program_id(0)

    @pl.when(i < N_PREP)
    def _():
        _prep_weights(i, (winf_ref, woutf_ref), (wino_ref, wouto_ref), (win_ref, wout_ref))

    @pl.when(i >= N_PREP)
    def _():
        _sgu_prompt_tile(x_ref, gain_ref, win_ref, lng_ref, lnb_ref, ws_ref, bsb_ref, wout_ref, y_ref, tm=tm)


def _sgu_prompt_tile(x_ref, gain_ref, win_ref, lng_ref, lnb_ref, ws_ref, bsb_ref, wout_ref,
                     y_ref, *, tm):
    ds = lng_ref.shape[-1]
    gw = ds // SGU_GROUPS
    x = x_ref[...]
    u, vn, z = _sgu_front(x, gain_ref, win_ref, lng_ref, lnb_ref)
    ri = lax.broadcasted_iota(jnp.int32, (SGU_CHUNK, SGU_CHUNK), 0)
    ci = lax.broadcasted_iota(jnp.int32, (SGU_CHUNK, SGU_CHUNK), 1)
    ws = [jnp.where(ri >= ci, ws_ref[g], 0.0).astype(BF16) for g in range(SGU_GROUPS)]
    vnb = vn.astype(BF16)
    bsb = bsb_ref[...]
    rows = []
    for n in range(tm // SGU_CHUNK):
        r0 = n * SGU_CHUNK
        cols = [_dot(ws[g], vnb[r0:r0 + SGU_CHUNK, g * gw:(g + 1) * gw]) for g in range(SGU_GROUPS)]
        rows.append(jnp.concatenate(cols, axis=1) + bsb)
    s = jnp.concatenate(rows, axis=0)
    gated = u * s * z
    y_ref[...] = _dot(gated.astype(BF16), wout_ref[...]) + x


def _sgu_prompt(x, gain, lng, lnb, ws, bsb, win, wout, *, tm):
    n, d = x.shape
    body = functools.partial(_sgu_prompt_body, tm=tm)
    w_in_specs, w_out_specs, w_out_shapes, w_scratch = zip(*[_prep_specs(w) for w in (win, wout)])
    tok = lambda i: (jnp.maximum(i - N_PREP, 0), 0)
    return pl.pallas_call(
        body,
        grid=(N_PREP + n // tm,),
        in_specs=[pl.BlockSpec((tm, d), tok)] + [
            _const_spec(a.shape) for a in (gain, lng, lnb, ws, bsb)] + [*w_in_specs],
        out_specs=[pl.BlockSpec((tm, d), tok), *w_out_specs],
        out_shape=[jax.ShapeDtypeStruct((n, d), F32), *w_out_shapes],
        scratch_shapes=[*w_scratch],
        compiler_params=_params(1),
        name="sgu_prompt",
    )(x, gain, lng, lnb, ws, bsb, win, wout)


def _sgu_sample_body(x_ref, gain_ref, win_ref, lng_ref, lnb_ref, w00_ref, b0_ref, wout_ref,
                     y_ref, v_ref):
    x = x_ref[...]
    u, vn, z = _sgu_front(x, gain_ref, win_ref, lng_ref, lnb_ref)
    s = vn * w00_ref[...] + b0_ref[...]
    gated = u * s * z
    y_ref[...] = _dot(gated.astype(BF16), wout_ref[...]) + x
    _store_rows(v_ref, vn)


def _sgu_sample(x, gain, win, lng, lnb, w00, b0, wout):
    nb, d = x.shape
    ds = lng.shape[-1]
    return pl.pallas_call(
        _sgu_sample_body,
        out_shape=[jax.ShapeDtypeStruct((nb, d), F32), jax.ShapeDtypeStruct((nb, 1, ds), F32)],
        compiler_params=_params(0),
        name="sgu_sample",
    )(x, gain, win, lng, lnb, w00, b0, wout)


def _gdn_gates(ab, alog_ref, dtb_ref):
    g = -jnp.exp(alog_ref[...]) * jax.nn.softplus(ab + dtb_ref[...])
    beta = jax.nn.sigmoid(ab)
    return g, beta


def _lane_pick(a, lane, idx):
    return jnp.sum(jnp.where(lane == idx, a, 0.0), axis=-1, keepdims=True)


def _gdn_heads(c, g_like, beta_all, emit):
    rows = c.shape[0]
    qk = DN_HEADS * DN_DK
    lane = lax.broadcasted_iota(jnp.int32, (rows, LANES), 1)
    for hh in range(DN_HEADS):
        qh = c[:, hh * DN_DK:(hh + 1) * DN_DK]
        kh = c[:, qk + hh * DN_DK:qk + (hh + 1) * DN_DK]
        vh = c[:, 2 * qk + hh * DN_DV:2 * qk + (hh + 1) * DN_DV]
        qn = qh * (lax.rsqrt(jnp.sum(qh * qh, axis=-1, keepdims=True) + EPS) * (DN_DK ** -0.5))
        kn = kh * lax.rsqrt(jnp.sum(kh * kh, axis=-1, keepdims=True) + EPS)
        emit(hh, qn, kn, vh, _lane_pick(g_like, lane, hh), _lane_pick(beta_all, lane, DN_HEADS + hh))


def _gdn_proj_prompt_body(x_ref, gain_ref, win_ref, cw_ref, alog_ref, dtb_ref,
                          q_ref, k_ref, kb_ref, vb_ref, gam_ref, z_ref, cst_ref, hist_ref,
                          *, tm, n_t):
    t = pl.program_id(1)
    qk = DN_HEADS * DN_DK
    cch = 3 * qk

    @pl.when(t == 0)
    def _():
        hist_ref[...] = jnp.zeros_like(hist_ref)

    x = x_ref[...]
    h = _rmsnorm(x, gain_ref[...]).astype(BF16)
    proj = _dot(h, win_ref[...])
    qkv = proj[:, :cch]
    z_ref[...] = proj[:, cch:cch + qk]
    ab = proj[:, cch + qk:cch + qk + LANES]
    ext = jnp.concatenate([hist_ref[...], qkv], axis=0)
    last = qkv[tm - CONV_HIST_PAD:, :]
    hist_ref[...] = last
    cw = cw_ref[...]
    conv = ext * cw[DN_CONV - 1:DN_CONV, :]
    for j in range(1, DN_CONV):
        conv = conv + pltpu.roll(ext, j, 0) * cw[DN_CONV - 1 - j:DN_CONV - j, :]
    c = _silu(conv[CONV_HIST_PAD:, :])

    g, beta = _gdn_gates(ab, alog_ref, dtb_ref)
    rin = lax.broadcasted_iota(jnp.int32, (tm, LANES), 0) & (DN_CHUNK - 1)
    gam = g
    sh = 1
    while sh < DN_CHUNK:
        gam = gam + jnp.where(rin >= sh, pltpu.roll(gam, sh, 0), 0.0)
        sh *= 2

    def emit(hh, qn, kn, vh, g_col, b_col):
        sl = slice(hh * DN_DK, (hh + 1) * DN_DK)
        q_ref[:, sl] = qn.astype(BF16)
        k_ref[:, sl] = kn.astype(BF16)
        kb_ref[:, sl] = (b_col * kn).astype(BF16)
        vb_ref[:, sl] = (b_col * vh).astype(BF16)
        gam_ref[:, sl] = jnp.broadcast_to(g_col, (tm, DN_DK))

    _gdn_heads(c, gam, beta, emit)

    @pl.when(t == n_t - 1)
    def _():
        cst_ref[...] = last


def _gdn_proj_prompt(x, gain, win, cw, alog, dtb, *, tm):
    b, t, d = x.shape
    qk = DN_HEADS * DN_DK
    cch = 3 * qk
    n_t = t // tm
    body = functools.partial(_gdn_proj_prompt_body, tm=tm, n_t=n_t)
    tok = lambda i, j: (i, j, 0)
    return pl.pallas_call(
        body,
        grid=(b, n_t),
        in_specs=[pl.BlockSpec((None, tm, d), tok)] + [
            _const_spec(a.shape) for a in (gain, win, cw, alog, dtb)],
        out_specs=[pl.BlockSpec((None, tm, qk), tok)] * 6 + [
            pl.BlockSpec((None, CONV_HIST_PAD, cch), lambda i, j: (i, 0, 0))],
        out_shape=[jax.ShapeDtypeStruct((b, t, qk), BF16)] * 4 + [
            jax.ShapeDtypeStruct((b, t, qk), F32)] * 2 + [
            jax.ShapeDtypeStruct((b, CONV_HIST_PAD, cch), F32)],
        scratch_shapes=[pltpu.VMEM((CONV_HIST_PAD, cch), F32)],
        compiler_params=_params(2),
        name="gdn_proj_prompt",
    )(x, gain, win, cw, alog, dtb)


def _unit_lower_inverse_minus_eye(ms, in_base, at_level):
    bf = lambda xs: [x.astype(BF16) for x in xs]
    a = [jnp.where(in_base, m, 0.0) for m in ms]
    e = [-x for x in a]
    ab = bf(a)
    p = [_dot(x, x) for x in ab]
    n = ms[0].shape[0]
    for it in range(INV_BASE_BITS - 1):
        pb = bf(p)
        if it < INV_BASE_BITS - 2:
            both = [_dot(jnp.concatenate([x, y], axis=0), y) for x, y in zip(bf(e), pb)]
            ep = [x[:n] for x in both]
            p_next = [x[n:] for x in both]
        else:
            ep = [_dot(x, y) for x, y in zip(bf(e), pb)]
        e = [x + y + z for x, y, z in zip(e, p, ep)]
        if it < INV_BASE_BITS - 2:
            p = p_next
    for mask in at_level:
        low = [jnp.where(mask, m, 0.0) for m in ms]
        eb = bf(e)
        x = [lo + _dot(lb, y) for lo, lb, y in zip(low, bf(low), eb)]
        ex = [_dot(y, xb) for y, xb in zip(eb, bf(x))]
        e = [ei - (xi + yi) for ei, xi, yi in zip(e, x, ex)]
    return e


def _gdn_out(o_heads, z, x, og_ref, wout_ref):
    normed = [oh * lax.rsqrt(jnp.mean(oh * oh, axis=-1, keepdims=True) + EPS) * og_ref[...]
              for oh in o_heads]
    gated = jnp.concatenate(normed, axis=1) * _silu(z)
    return _dot(gated.astype(BF16), wout_ref[...]) + x


def _gdn_delta_prompt_body(q_ref, k_ref, kb_ref, vb_ref, gam_ref, z_ref, x_ref, og_ref, wout_ref,
                           y_ref, sfin_ref, s_ref, o_ref, *, tm, n_t):
    t = pl.program_id(1)

    @pl.when(t == 0)
    def _():
        s_ref[...] = jnp.zeros_like(s_ref)

    c = DN_CHUNK
    ri = lax.broadcasted_iota(jnp.int32, (c, c), 0)
    ci = lax.broadcasted_iota(jnp.int32, (c, c), 1)
    incl = ri >= ci
    strict = ri > ci
    top = ri ^ ci
    in_base = (top >> INV_BASE_BITS) == 0
    levels = range(INV_BASE_BITS, c.bit_length() - 1)
    at_level = [(top >> lv) == 1 for lv in levels]
    n_chunks = tm // c
    pairs = [(n, hh) for n in range(n_chunks) for hh in range(DN_HEADS)]

    def blk(ref, n, hh):
        return ref[n * c:(n + 1) * c, hh * DN_DK:(hh + 1) * DN_DK]

    kh = [blk(k_ref, *p) for p in pairs]
    kbh = [blk(kb_ref, *p) for p in pairs]
    gam = [blk(gam_ref, *p) for p in pairs]
    qh = [blk(q_ref, *p) for p in pairs]
    kq = [_dot_nt(jnp.concatenate([a, q], axis=0), b) for a, q, b in zip(kbh, qh, kh)]
    kk = [x[:c] for x in kq]
    dec = [jnp.exp(jnp.where(incl, g - g.T, -1e30)) for g in gam]
    m = [jnp.where(strict, a * d, 0.0) for a, d in zip(kk, dec)]
    e = _unit_lower_inverse_minus_eye(m, in_base, at_level)
    eg = [jnp.exp(g) for g in gam]
    rhs = [jnp.concatenate([blk(vb_ref, *p).astype(F32), a.astype(F32) * x], axis=1)
           for p, a, x in zip(pairs, kbh, eg)]
    uw = [r + _dot(x.astype(BF16), r.astype(BF16)) for r, x in zip(rhs, e)]
    qkm = [(x[c:] * d).astype(BF16) for x, d in zip(kq, dec)]
    qd = [(a.astype(F32) * x).astype(BF16) for a, x in zip(qh, eg)]
    gl = [g[c - 1:c, :] for g in gam]
    kt = [(a.astype(F32) * jnp.exp(l - g)).astype(BF16) for a, l, g in zip(kh, gl, gam)]
    al = [jnp.exp(l) for l in gl]

    for n in range(n_chunks):
        idx = [n * DN_HEADS + hh for hh in range(DN_HEADS)]
        s_old = [s_ref[hh] for hh in range(DN_HEADS)]
        sb = [x.astype(BF16) for x in s_old]
        ws = [_dot(jnp.concatenate([uw[i][:, DN_DV:].astype(BF16), qd[i]], axis=0), y)
              for i, y in zip(idx, sb)]
        w = [uw[i][:, :DN_DV] - x[:c] for i, x in zip(idx, ws)]
        wb = [x.astype(BF16) for x in w]
        for hh, i in enumerate(idx):
            o_ref[n * c:(n + 1) * c, hh * DN_DV:(hh + 1) * DN_DV] = ws[hh][c:] + _dot(qkm[i], wb[hh])
        for hh, i in enumerate(idx):
            s_ref[hh] = al[i] * s_old[hh] + _dot_tn(kt[i], wb[hh])
    o = o_ref[...]
    o_heads = [o[:, hh * DN_DV:(hh + 1) * DN_DV] for hh in range(DN_HEADS)]
    y_ref[...] = _gdn_out(o_heads, z_ref[...], x_ref[...], og_ref, wout_ref)

    @pl.when(t == n_t - 1)
    def _():
        sfin_ref[...] = s_ref[...]


def _gdn_delta_prompt(q, k, kb, vb, gam, z, x, og, wout, *, tm):
    b, t, d = x.shape
    qk = DN_HEADS * DN_DK
    n_t = t // tm
    body = functools.partial(_gdn_delta_prompt_body, tm=tm, n_t=n_t)
    tok = lambda i, j: (i, j, 0)
    return pl.pallas_call(
        body,
        grid=(b, n_t),
        in_specs=[pl.BlockSpec((None, tm, qk), tok)] * 6 + [pl.BlockSpec((None, tm, d), tok),
                  _const_spec(og.shape), _const_spec(wout.shape)],
        out_specs=[pl.BlockSpec((None, tm, d), tok),
                   pl.BlockSpec((None, DN_HEADS, DN_DK, DN_DV), lambda i, j: (i, 0, 0, 0))],
        out_shape=[jax.ShapeDtypeStruct((b, t, d), F32),
                   jax.ShapeDtypeStruct((b, DN_HEADS, DN_DK, DN_DV), F32)],
        scratch_shapes=[pltpu.VMEM((DN_HEADS, DN_DK, DN_DV), F32),
                        pltpu.VMEM((tm, DN_HEADS * DN_DV), F32)],
        compiler_params=_params(2),
        name="gdn_delta_prompt",
    )(q, k, kb, vb, gam, z, x, og, wout)


def _gdn_proj_sample_body(x_ref, cst_ref, gain_ref, win_ref, cw_ref, alog_ref, dtb_ref,
                          q_ref, k_ref, v_ref, eg_ref, beta_ref, z_ref, ncst_ref):
    qk = DN_HEADS * DN_DK
    cch = 3 * qk
    nb = x_ref.shape[0]
    x = x_ref[...]
    h = _rmsnorm(x, gain_ref[...]).astype(BF16)
    proj = _dot(h, win_ref[...])
    qkv = proj[:, :cch]
    z_ref[...] = proj[:, cch:cch + qk]
    ab = proj[:, cch + qk:cch + qk + LANES]
    cw = cw_ref[...]
    conv = qkv * cw[DN_CONV - 1:DN_CONV, :]
    for j in range(DN_CONV - 1):
        conv = conv + cst_ref[j] * cw[j:j + 1, :]
    c = _silu(conv)
    ncst_ref[:DN_CONV - 2] = cst_ref[1:]
    ncst_ref[DN_CONV - 2] = qkv
    g, beta = _gdn_gates(ab, alog_ref, dtb_ref)

    def emit(hh, qn, kn, vh, g_col, b_col):
        q_ref[:, hh, :] = qn
        k_ref[:, hh, :] = kn
        v_ref[:, hh, :] = vh
        eg_ref[:, hh, :] = jnp.broadcast_to(jnp.exp(g_col), (nb, DN_DK))
        beta_ref[:, hh, :] = jnp.broadcast_to(b_col, (nb, DN_DK))

    _gdn_heads(c, g, beta, emit)


def _gdn_proj_sample(x, cst, gain, win, cw, alog, dtb):
    nb, d = x.shape
    qk = DN_HEADS * DN_DK
    return pl.pallas_call(
        _gdn_proj_sample_body,
        out_shape=[jax.ShapeDtypeStruct((nb, DN_HEADS, DN_DK), F32)] * 5 + [
            jax.ShapeDtypeStruct((nb, qk), F32), jax.ShapeDtypeStruct(cst.shape, F32)],
        compiler_params=_params(0),
        name="gdn_proj_sample",
    )(x, cst, gain, win, cw, alog, dtb)


def _gdn_state_sample_body(q_ref, k_ref, v_ref, eg_ref, beta_ref, s_ref, o_ref, ns_ref, *, nblk):
    n_pair = DN_HEADS // 2
    row = lax.broadcasted_iota(jnp.int32, (DN_HEADS, n_pair * DN_DK), 0)
    col_pair = lax.broadcasted_iota(jnp.int32, (DN_HEADS, n_pair * DN_DK), 1) // DN_DK
    in_pair = (row // 2) == col_pair
    even = (lax.broadcasted_iota(jnp.int32, (DN_HEADS, DN_DV), 0) % 2) == 0

    def by_pair(a):
        return jnp.where(in_pair, jnp.concatenate([a] * n_pair, axis=1), 0.0).astype(BF16)

    def paired(s):
        return jnp.concatenate(
            [jnp.concatenate([s[2 * p], s[2 * p + 1]], axis=1) for p in range(n_pair)], axis=0)

    def own_half(a):
        return jnp.where(even, a[:, :DN_DV], a[:, DN_DV:])

    seqs = range(nblk)
    s_old = [[s_ref[r, hh] for hh in range(DN_HEADS)] for r in seqs]
    kp = [by_pair(k_ref[r]) for r in seqs]
    eg = [eg_ref[r] for r in seqs]
    kts = [own_half(_dot(kp[r], paired(s_old[r]).astype(BF16))) for r in seqs]
    w = [beta_ref[r] * (v_ref[r] - eg[r] * kts[r]) for r in seqs]
    w2 = [jnp.concatenate([jnp.where(even, x, 0.0), jnp.where(even, 0.0, x)], axis=1) for x in w]
    outer = [_dot_tn(kp[r], w2[r].astype(BF16)) for r in seqs]
    s_new = []
    for r in seqs:
        s_r = []
        for hh in range(DN_HEADS):
            p, b = divmod(hh, 2)
            blk = outer[r][p * DN_DK:(p + 1) * DN_DK, b * DN_DV:(b + 1) * DN_DV]
            s_r.append(s_old[r][hh] * eg[r][hh:hh + 1, :] + blk)
            ns_ref[r, hh] = s_r[hh]
        s_new.append(s_r)
    for r in seqs:
        o_ref[r] = own_half(_dot(by_pair(q_ref[r]), paired(s_new[r]).astype(BF16)))


def _gdn_state_sample(q, k, v, eg, beta, s, *, nblk):
    nb = s.shape[0]
    body = functools.partial(_gdn_state_sample_body, nblk=nblk)
    vec = pl.BlockSpec((nblk, DN_HEADS, DN_DK), lambda i: (i, 0, 0))
    st = pl.BlockSpec((nblk, DN_HEADS, DN_DK, DN_DV), lambda i: (i, 0, 0, 0))
    return pl.pallas_call(
        body,
        grid=(nb // nblk,),
        in_specs=[vec] * 5 + [st],
        out_specs=[vec, st],
        out_shape=[jax.ShapeDtypeStruct((nb, DN_HEADS, DN_DV), F32),
                   jax.ShapeDtypeStruct(s.shape, F32)],
        compiler_params=_params(1),
        name="gdn_state_sample",
    )(q, k, v, eg, beta, s)


def _gdn_out_sample_body(o_ref, z_ref, x_ref, og_ref, wout_ref, y_ref):
    o_heads = [o_ref[:, hh, :] for hh in range(DN_HEADS)]
    y_ref[...] = _gdn_out(o_heads, z_ref[...], x_ref[...], og_ref, wout_ref)


def _gdn_out_sample(o, z, x, og, wout):
    return pl.pallas_call(
        _gdn_out_sample_body,
        out_shape=jax.ShapeDtypeStruct(x.shape, F32),
        compiler_params=_params(0),
        name="gdn_out_sample",
    )(o, z, x, og, wout)


def _row(a):
    return a.reshape(1, -1).astype(F32)


def _time_major(st):
    return jnp.transpose(st, (1, 0, 2))


def _forward(x_prompt, x_sample, state_pool_l0, state_conv_l2, state_delta_l2, state_pool_l3,
             l0_norm, l0_pool_w_in, l0_pool_w_grp, l0_pool_scale, l0_pool_w_out,
             l1_norm, l1_sgu_w_in, l1_sgu_ln_g, l1_sgu_ln_b, l1_sgu_w_s, l1_sgu_b_s, l1_sgu_w_out,
             l2_norm, l2_dn_w_in, l2_dn_conv_w, l2_dn_a_log, l2_dn_dt_bias, l2_dn_o_gain, l2_dn_w_out,
             l3_norm, l3_pool_w_in, l3_pool_w_grp, l3_pool_scale, l3_pool_w_out,
             final_norm, *, tm_pool, tm_sgu, tm_gdn_proj, tm_gdn, nblk_state):
    b, t, d = x_prompt.shape
    nb = x_sample.shape[0]
    dp = l0_pool_scale.shape[0]
    ds = l1_sgu_ln_g.shape[0]
    gw_s = ds // SGU_GROUPS
    qk = DN_HEADS * DN_DK
    cch = 3 * qk
    fin = _row(final_norm)
    xs = x_sample

    def pool_layer(xp, xs_, st, norm, w_in, w_grp, scale, w_out, final):
        gain, sc = _row(norm), _row(scale)
        yp, st_p, win_b, wgrp_b, wout_b = _pool_prompt(
            xp, gain, sc, fin, w_in, w_grp.reshape(-1, w_grp.shape[-1]), w_out,
            tm=tm_pool, final_norm=final)
        ys, st_s = _pool_sample(xs_, _time_major(st), gain, win_b, wgrp_b, sc, wout_b, fin,
                                final_norm=final, y_shape=(nb, 1, d) if final else (nb, d))
        return yp, ys, st_p[:, POOL_HIST_PAD - POOL_HIST:, :], _time_major(st_s)

    xp, xs, pool0_p, pool0_s = pool_layer(
        x_prompt, xs, state_pool_l0, l0_norm, l0_pool_w_in, l0_pool_w_grp, l0_pool_scale,
        l0_pool_w_out, False)

    bsb = jnp.repeat(l1_sgu_b_s.T.astype(F32), gw_s, axis=1)
    xp, sgu_w_in, sgu_w_out = _sgu_prompt(
        xp.reshape(b * t, d), _row(l1_norm), _row(l1_sgu_ln_g), _row(l1_sgu_ln_b),
        l1_sgu_w_s.astype(F32), bsb, l1_sgu_w_in, l1_sgu_w_out, tm=tm_sgu)
    xp = xp.reshape(b, t, d)
    w00 = jnp.repeat(l1_sgu_w_s[:, 0, 0].astype(F32), gw_s)[None, :]
    b0 = jnp.repeat(l1_sgu_b_s[:, 0].astype(F32), gw_s)[None, :]
    xs, sgu1_s = _sgu_sample(xs, _row(l1_norm), sgu_w_in, _row(l1_sgu_ln_g), _row(l1_sgu_ln_b),
                             w00, b0, sgu_w_out)

    n_proj = l2_dn_w_in.shape[1]
    n_pad = cch + qk + LANES
    dn_w_in = jnp.pad(l2_dn_w_in, ((0, 0), (0, n_pad - n_proj))).astype(BF16)
    dn_w_out = l2_dn_w_out.astype(BF16)
    alog = jnp.pad(l2_dn_a_log.astype(F32), (0, LANES - DN_HEADS))[None, :]
    dtb = jnp.pad(l2_dn_dt_bias.astype(F32), (0, LANES - DN_HEADS))[None, :]
    cw = l2_dn_conv_w.astype(F32)
    og = _row(l2_dn_o_gain)
    q, k, kb, vb, gam, z, conv2_p = _gdn_proj_prompt(xp, _row(l2_norm), dn_w_in, cw, alog, dtb,
                                                     tm=tm_gdn_proj)
    xp, dn2_p = _gdn_delta_prompt(q, k, kb, vb, gam, z, xp, og, dn_w_out, tm=tm_gdn)
    conv2_p = conv2_p[:, CONV_HIST_PAD - (DN_CONV - 1):, :]

    qs, ks, vs, egs, betas, zs, conv2_s = _gdn_proj_sample(
        xs, _time_major(state_conv_l2), _row(l2_norm), dn_w_in, cw, alog, dtb)
    conv2_s = _time_major(conv2_s)
    o_s, dn2_s = _gdn_state_sample(qs, ks, vs, egs, betas, state_delta_l2, nblk=nblk_state)
    xs = _gdn_out_sample(o_s, zs, xs, og, dn_w_out)

    yp, ys, pool3_p, pool3_s = pool_layer(
        xp, xs, state_pool_l3, l3_norm, l3_pool_w_in, l3_pool_w_grp, l3_pool_scale,
        l3_pool_w_out, True)

    return (yp, ys, pool0_p, pool0_s, sgu1_s,
            conv2_p, conv2_s, dn2_p, dn2_s, pool3_p, pool3_s)


def kernel(x_prompt, x_sample, state_pool_l0, state_conv_l2, state_delta_l2, state_pool_l3, l0_norm, l0_pool_w_in, l0_pool_w_grp, l0_pool_scale, l0_pool_w_out, l1_norm, l1_sgu_w_in, l1_sgu_ln_g, l1_sgu_ln_b, l1_sgu_w_s, l1_sgu_b_s, l1_sgu_w_out, l2_norm, l2_dn_w_in, l2_dn_conv_w, l2_dn_a_log, l2_dn_dt_bias, l2_dn_o_gain, l2_dn_w_out, l3_norm, l3_pool_w_in, l3_pool_w_grp, l3_pool_scale, l3_pool_w_out, final_norm):
    return _forward(
        x_prompt, x_sample, state_pool_l0, state_conv_l2, state_delta_l2, state_pool_l3,
        l0_norm, l0_pool_w_in, l0_pool_w_grp, l0_pool_scale, l0_pool_w_out,
        l1_norm, l1_sgu_w_in, l1_sgu_ln_g, l1_sgu_ln_b, l1_sgu_w_s, l1_sgu_b_s, l1_sgu_w_out,
        l2_norm, l2_dn_w_in, l2_dn_conv_w, l2_dn_a_log, l2_dn_dt_bias, l2_dn_o_gain, l2_dn_w_out,
        l3_norm, l3_pool_w_in, l3_pool_w_grp, l3_pool_scale, l3_pool_w_out,
        final_norm, tm_pool=1024, tm_sgu=512, tm_gdn_proj=512, tm_gdn=256, nblk_state=8)
```

```python
import functools

import jax
import jax.numpy as jnp
from jax import lax
from jax.experimental import pallas as pl
from jax.experimental.pallas import tpu as pltpu

F32 = jnp.float32
BF16 = jnp.bfloat16
EPS = 1e-6
PAST_LEN = 16384
POOL_WINDOWS = (2, 4, 8, 16)
POOL_HIST = 15
POOL_HIST_PAD = 16
SGU_CHUNK = 128
SGU_GROUPS = 4
DN_HEADS = 8
DN_DK = 128
DN_DV = 128
DN_CONV = 4
CONV_HIST_PAD = 8
DN_CHUNK = 128
INV_BASE_BITS = 4
LANES = 128
N_PREP = 8
VMEM_LIMIT = 60 * 1024 * 1024


def _rmsnorm(x, gain):
    ms = jnp.mean(x * x, axis=-1, keepdims=True)
    return x * lax.rsqrt(ms + EPS) * gain


def _silu_of_half(h):
    return h + h * jnp.tanh(h)


def _silu(z):
    return _silu_of_half(0.5 * z)


def _gelu(x):
    return 0.5 * x * (1.0 + lax.erf(x * (2.0 ** -0.5)))


def _dot(a, b):
    return jnp.dot(a, b, preferred_element_type=F32)


def _dot_nt(a, b):
    return lax.dot_general(a, b, (((1,), (1,)), ((), ())), preferred_element_type=F32)


def _dot_tn(a, b):
    return lax.dot_general(a, b, (((0,), (0,)), ((), ())), preferred_element_type=F32)


def _load_rows(ref):
    return ref[:, 0, :] if len(ref.shape) == 3 else ref[...]


def _store_rows(ref, val):
    if len(ref.shape) == 3:
        ref[:, 0, :] = val
    else:
        ref[...] = val


def _const_spec(shape):
    nd = len(shape)
    return pl.BlockSpec(shape, lambda *_: (0,) * nd, pipeline_mode=pl.Buffered(1))


def _params(n_grid):
    return pltpu.CompilerParams(
        dimension_semantics=("arbitrary",) * n_grid, vmem_limit_bytes=VMEM_LIMIT)


def _prep_specs(w):
    rows, cols = w.shape
    chunk = (rows // N_PREP, cols)
    index = lambda i: (jnp.minimum(i, N_PREP - 1), 0)
    return (pl.BlockSpec(chunk, index), pl.BlockSpec(chunk, index),
            jax.ShapeDtypeStruct(w.shape, BF16), pltpu.VMEM(w.shape, BF16))


def _prep_weights(step, f32_refs, out_refs, resident_refs):
    for f_ref, o_ref, r_ref in zip(f32_refs, out_refs, resident_refs):
        rows = f_ref.shape[0]
        chunk = f_ref[...].astype(BF16)
        o_ref[...] = chunk
        r_ref[pl.ds(pl.multiple_of(step * rows, rows), rows), :] = chunk


def _token_step(i, n_t):
    j = jnp.maximum(i - N_PREP, 0)
    return j // n_t, j % n_t


def _pool_tail(pooled_groups, z, x, wgrp_ref, scale_ref, wout_ref, fin_ref, final_norm):
    gw = pooled_groups[0].shape[1]
    mixed = [_dot(p.astype(BF16), wgrp_ref[gi * gw:(gi + 1) * gw, :]) for gi, p in enumerate(pooled_groups)]
    mixed = jnp.concatenate(mixed, axis=1)
    gated = mixed * scale_ref[...] * _silu(z)
    y = _dot(gated.astype(BF16), wout_ref[...]) + x
    if final_norm:
        y = _rmsnorm(y, fin_ref[...])
    return y


def _pool_prompt_body(x_ref, gain_ref, scale_ref, fin_ref, winf_ref, wgrpf_ref, woutf_ref,
                      y_ref, st_ref, wino_ref, wgrpo_ref, wouto_ref,
                      win_ref, wgrp_ref, wout_ref, hist_ref, *, tm, n_t, final_norm):
    i = pl.program_id(0)

    @pl.when(i < N_PREP)
    def _():
        _prep_weights(i, (winf_ref, wgrpf_ref, woutf_ref), (wino_ref, wgrpo_ref, wouto_ref),
                      (win_ref, wgrp_ref, wout_ref))

    @pl.when(i >= N_PREP)
    def _():
        _pool_prompt_tile(_token_step(i, n_t)[1], x_ref, gain_ref, scale_ref, fin_ref, y_ref, st_ref,
                          win_ref, wgrp_ref, wout_ref, hist_ref, tm=tm, n_t=n_t, final_norm=final_norm)


def _pool_prompt_tile(t, x_ref, gain_ref, scale_ref, fin_ref, y_ref, st_ref,
                      win_ref, wgrp_ref, wout_ref, hist_ref, *, tm, n_t, final_norm):
    dp = scale_ref.shape[-1]
    gw = dp // len(POOL_WINDOWS)

    @pl.when(t == 0)
    def _():
        hist_ref[...] = jnp.zeros_like(hist_ref)

    x = x_ref[...]
    h = _rmsnorm(x, gain_ref[...]).astype(BF16)
    xz = _dot(h, win_ref[...])
    xb = xz[:, :dp]
    z = xz[:, dp:]
    ext = jnp.concatenate([hist_ref[...], xb], axis=0)
    last = xb[tm - POOL_HIST_PAD:, :]
    hist_ref[...] = last
    pos1 = lax.broadcasted_iota(jnp.int32, (tm, LANES), 0) + (t * tm + 1)
    pooled = []
    for gi, w in enumerate(POOL_WINDOWS):
        e = ext[:, gi * gw:(gi + 1) * gw]
        s = e + pltpu.roll(e, 1, 0)
        sh = 2
        while sh < w:
            s = s + pltpu.roll(s, sh, 0)
            sh *= 2
        s = s[POOL_HIST_PAD:, :]
        inv = 1.0 / jnp.minimum(pos1, w).astype(F32)
        inv = jnp.concatenate([inv] * (gw // LANES), axis=1)
        pooled.append(s * inv - xb[:, gi * gw:(gi + 1) * gw])
    y_ref[...] = _pool_tail(pooled, z, x, wgrp_ref, scale_ref, wout_ref, fin_ref, final_norm)

    @pl.when(t == n_t - 1)
    def _():
        st_ref[...] = last


def _pool_prompt(x, gain, scale, fin, win, wgrp, wout, *, tm, final_norm):
    b, t, d = x.shape
    dp = scale.shape[-1]
    n_t = t // tm
    body = functools.partial(_pool_prompt_body, tm=tm, n_t=n_t, final_norm=final_norm)
    w_in_specs, w_out_specs, w_out_shapes, w_scratch = zip(*[_prep_specs(w) for w in (win, wgrp, wout)])
    tok = lambda i: _token_step(i, n_t) + (0,)
    return pl.pallas_call(
        body,
        grid=(N_PREP + b * n_t,),
        in_specs=[pl.BlockSpec((None, tm, d), tok), _const_spec(gain.shape), _const_spec(scale.shape),
                  _const_spec(fin.shape), *w_in_specs],
        out_specs=[pl.BlockSpec((None, tm, d), tok),
                   pl.BlockSpec((None, POOL_HIST_PAD, dp), lambda i: (_token_step(i, n_t)[0], 0, 0)),
                   *w_out_specs],
        out_shape=[jax.ShapeDtypeStruct((b, t, d), F32),
                   jax.ShapeDtypeStruct((b, POOL_HIST_PAD, dp), F32), *w_out_shapes],
        scratch_shapes=[*w_scratch, pltpu.VMEM((POOL_HIST_PAD, dp), F32)],
        compiler_params=_params(1),
        name="pool_prompt",
    )(x, gain, scale, fin, win, wgrp, wout)


def _pool_sample_body(x_ref, st_ref, gain_ref, win_ref, wgrp_ref, scale_ref, wout_ref, fin_ref,
                      y_ref, nst_ref, *, final_norm):
    dp = scale_ref.shape[-1]
    gw = dp // len(POOL_WINDOWS)
    x = _load_rows(x_ref)
    h = _rmsnorm(x, gain_ref[...]).astype(BF16)
    xz = _dot(h, win_ref[...])
    xb = xz[:, :dp]
    z = xz[:, dp:]
    pooled = []
    for gi, w in enumerate(POOL_WINDOWS):
        xg = xb[:, gi * gw:(gi + 1) * gw]
        s = xg
        for k in range(1, w):
            s = s + st_ref[POOL_HIST - k, :, gi * gw:(gi + 1) * gw]
        cnt = float(min(PAST_LEN + 1, w))
        pooled.append(s * (1.0 / cnt) - xg)
    _store_rows(y_ref, _pool_tail(pooled, z, x, wgrp_ref, scale_ref, wout_ref, fin_ref, final_norm))
    nst_ref[:POOL_HIST - 1] = st_ref[1:]
    nst_ref[POOL_HIST - 1] = xb


def _pool_sample(x, st, gain, win, wgrp, scale, wout, fin, *, final_norm, y_shape):
    body = functools.partial(_pool_sample_body, final_norm=final_norm)
    return pl.pallas_call(
        body,
        out_shape=[jax.ShapeDtypeStruct(y_shape, F32), jax.ShapeDtypeStruct(st.shape, F32)],
        compiler_params=_params(0),
        name="pool_sample",
    )(x, st, gain, win, wgrp, scale, wout, fin)


def _sgu_front(x, gain_ref, win_ref, lng_ref, lnb_ref):
    ds = lng_ref.shape[-1]
    h = _rmsnorm(x, gain_ref[...]).astype(BF16)
    uvz = _dot(h, win_ref[...])
    u = _gelu(uvz[:, :ds])
    v = _gelu(uvz[:, ds:2 * ds])
    z = _silu(uvz[:, 2 * ds:])
    mu = jnp.mean(v, axis=-1, keepdims=True)
    vc = v - mu
    var = jnp.mean(vc * vc, axis=-1, keepdims=True)
    vn = vc * lax.rsqrt(var + EPS) * lng_ref[...] + lnb_ref[...]
    return u, vn, z


def _sgu_prompt_body(x_ref, gain_ref, lng_ref, lnb_ref, ws_ref, bsb_ref, winf_ref, woutf_ref,
                     y_ref, wino_ref, wouto_ref, win_ref, wout_ref, *, tm):
    i = pl.program_id(0)

    @pl.when(i < N_PREP)
    def _():
        _prep_weights(i, (winf_ref, woutf_ref), (wino_ref, wouto_ref), (win_ref, wout_ref))

    @pl.when(i >= N_PREP)
    def _():
        _sgu_prompt_tile(x_ref, gain_ref, win_ref, lng_ref, lnb_ref, ws_ref, bsb_ref, wout_ref, y_ref, tm=tm)


def _sgu_prompt_tile(x_ref, gain_ref, win_ref, lng_ref, lnb_ref, ws_ref, bsb_ref, wout_ref,
                     y_ref, *, tm):
    ds = lng_ref.shape[-1]
    gw = ds // SGU_GROUPS
    x = x_ref[...]
    u, vn, z = _sgu_front(x, gain_ref, win_ref, lng_ref, lnb_ref)
    ri = lax.broadcasted_iota(jnp.int32, (SGU_CHUNK, SGU_CHUNK), 0)
    ci = lax.broadcasted_iota(jnp.int32, (SGU_CHUNK, SGU_CHUNK), 1)
    ws = [jnp.where(ri >= ci, ws_ref[g], 0.0).astype(BF16) for g in range(SGU_GROUPS)]
    vnb = vn.astype(BF16)
    bsb = bsb_ref[...]
    rows = []
    for n in range(tm // SGU_CHUNK):
        r0 = n * SGU_CHUNK
        cols = [_dot(ws[g], vnb[r0:r0 + SGU_CHUNK, g * gw:(g + 1) * gw]) for g in range(SGU_GROUPS)]
        rows.append(jnp.concatenate(cols, axis=1) + bsb)
    s = jnp.concatenate(rows, axis=0)
    gated = u * s * z
    y_ref[...] = _dot(gated.astype(BF16), wout_ref[...]) + x


def _sgu_prompt(x, gain, lng, lnb, ws, bsb, win, wout, *, tm):
    n, d = x.shape
    body = functools.partial(_sgu_prompt_body, tm=tm)
    w_in_specs, w_out_specs, w_out_shapes, w_scratch = zip(*[_prep_specs(w) for w in (win, wout)])
    tok = lambda i: (jnp.maximum(i - N_PREP, 0), 0)
    return pl.pallas_call(
        body,
        grid=(N_PREP + n // tm,),
        in_specs=[pl.BlockSpec((tm, d), tok)] + [
            _const_spec(a.shape) for a in (gain, lng, lnb, ws, bsb)] + [*w_in_specs],
        out_specs=[pl.BlockSpec((tm, d), tok), *w_out_specs],
        out_shape=[jax.ShapeDtypeStruct((n, d), F32), *w_out_shapes],
        scratch_shapes=[*w_scratch],
        compiler_params=_params(1),
        name="sgu_prompt",
    )(x, gain, lng, lnb, ws, bsb, win, wout)


def _sgu_sample_body(x_ref, gain_ref, win_ref, lng_ref, lnb_ref, w00_ref, b0_ref, wout_ref,
                     y_ref, v_ref):
    x = x_ref[...]
    u, vn, z = _sgu_front(x, gain_ref, win_ref, lng_ref, lnb_ref)
    s = vn * w00_ref[...] + b0_ref[...]
    gated = u * s * z
    y_ref[...] = _dot(gated.astype(BF16), wout_ref[...]) + x
    _store_rows(v_ref, vn)


def _sgu_sample(x, gain, win, lng, lnb, w00, b0, wout):
    nb, d = x.shape
    ds = lng.shape[-1]
    return pl.pallas_call(
        _sgu_sample_body,
        out_shape=[jax.ShapeDtypeStruct((nb, d), F32), jax.ShapeDtypeStruct((nb, 1, ds), F32)],
        compiler_params=_params(0),
        name="sgu_sample",
    )(x, gain, win, lng, lnb, w00, b0, wout)


def _gdn_gates(ab, alog_ref, dtb_ref):
    g = -jnp.exp(alog_ref[...]) * jax.nn.softplus(ab + dtb_ref[...])
    beta = jax.nn.sigmoid(ab)
    return g, beta


def _lane_pick(a, lane, idx):
    return jnp.sum(jnp.where(lane == idx, a, 0.0), axis=-1, keepdims=True)


def _gdn_heads(c, g_like, beta_all, emit):
    rows = c.shape[0]
    qk = DN_HEADS * DN_DK
    lane = lax.broadcasted_iota(jnp.int32, (rows, LANES), 1)
    for hh in range(DN_HEADS):
        qh = c[:, hh * DN_DK:(hh + 1) * DN_DK]
        kh = c[:, qk + hh * DN_DK:qk + (hh + 1) * DN_DK]
        vh = c[:, 2 * qk + hh * DN_DV:2 * qk + (hh + 1) * DN_DV]
        qn = qh * (lax.rsqrt(jnp.sum(qh * qh, axis=-1, keepdims=True) + EPS) * (DN_DK ** -0.5))
        kn = kh * lax.rsqrt(jnp.sum(kh * kh, axis=-1, keepdims=True) + EPS)
        emit(hh, qn, kn, vh, _lane_pick(g_like, lane, hh), _lane_pick(beta_all, lane, DN_HEADS + hh))


def _gdn_proj_prompt_body(x_ref, gain_ref, win_ref, cw_ref, alog_ref, dtb_ref,
                          q_ref, k_ref, kb_ref, vb_ref, gam_ref, z_ref, cst_ref, hist_ref,
                          *, tm, n_t):
    t = pl.program_id(1)
    qk = DN_HEADS * DN_DK
    cch = 3 * qk

    @pl.when(t == 0)
    def _():
        hist_ref[...] = jnp.zeros_like(hist_ref)

    x = x_ref[...]
    h = _rmsnorm(x, gain_ref[...]).astype(BF16)
    proj = _dot(h, win_ref[...])
    qkv = proj[:, :cch]
    z_ref[...] = proj[:, cch:cch + qk]
    ab = proj[:, cch + qk:cch + qk + LANES]
    ext = jnp.concatenate([hist_ref[...], qkv], axis=0)
    last = qkv[tm - CONV_HIST_PAD:, :]
    hist_ref[...] = last
    cw = 0.5 * cw_ref[...]
    conv = ext * cw[DN_CONV - 1:DN_CONV, :]
    for j in range(1, DN_CONV):
        conv = conv + pltpu.roll(ext, j, 0) * cw[DN_CONV - 1 - j:DN_CONV - j, :]
    c = _silu_of_half(conv[CONV_HIST_PAD:, :])

    g, beta = _gdn_gates(ab, alog_ref, dtb_ref)
    rin = lax.broadcasted_iota(jnp.int32, (tm, LANES), 0) & (DN_CHUNK - 1)
    gam = g
    sh = 1
    while sh < DN_CHUNK:
        gam = gam + jnp.where(rin >= sh, pltpu.roll(gam, sh, 0), 0.0)
        sh *= 2

    def emit(hh, qn, kn, vh, g_col, b_col):
        sl = slice(hh * DN_DK, (hh + 1) * DN_DK)
        q_ref[:, sl] = qn.astype(BF16)
        k_ref[:, sl] = kn.astype(BF16)
        kb_ref[:, sl] = (b_col * kn).astype(BF16)
        vb_ref[:, sl] = (b_col * vh).astype(BF16)
        gam_ref[:, sl] = jnp.broadcast_to(g_col, (tm, DN_DK))

    _gdn_heads(c, gam, beta, emit)

    @pl.when(t == n_t - 1)
    def _():
        cst_ref[...] = last


def _gdn_proj_prompt(x, gain, win, cw, alog, dtb, *, tm):
    b, t, d = x.shape
    qk = DN_HEADS * DN_DK
    cch = 3 * qk
    n_t = t // tm
    body = functools.partial(_gdn_proj_prompt_body, tm=tm, n_t=n_t)
    tok = lambda i, j: (i, j, 0)
    return pl.pallas_call(
        body,
        grid=(b, n_t),
        in_specs=[pl.BlockSpec((None, tm, d), tok)] + [
            _const_spec(a.shape) for a in (gain, win, cw, alog, dtb)],
        out_specs=[pl.BlockSpec((None, tm, qk), tok)] * 6 + [
            pl.BlockSpec((None, CONV_HIST_PAD, cch), lambda i, j: (i, 0, 0))],
        out_shape=[jax.ShapeDtypeStruct((b, t, qk), BF16)] * 4 + [
            jax.ShapeDtypeStruct((b, t, qk), F32)] * 2 + [
            jax.ShapeDtypeStruct((b, CONV_HIST_PAD, cch), F32)],
        scratch_shapes=[pltpu.VMEM((CONV_HIST_PAD, cch), F32)],
        compiler_params=_params(2),
        name="gdn_proj_prompt",
    )(x, gain, win, cw, alog, dtb)


def _unit_lower_inverse_minus_eye(ms, in_base, at_level):
    bf = lambda xs: [x.astype(BF16) for x in xs]
    a = [jnp.where(in_base, m, 0.0) for m in ms]
    e = [-x for x in a]
    ab = bf(a)
    p = [_dot(x, x) for x in ab]
    n = ms[0].shape[0]
    for it in range(INV_BASE_BITS - 1):
        pb = bf(p)
        if it < INV_BASE_BITS - 2:
            both = [_dot(jnp.concatenate([x, y], axis=0), y) for x, y in zip(bf(e), pb)]
            ep = [x[:n] for x in both]
            p_next = [x[n:] for x in both]
        else:
            ep = [_dot(x, y) for x, y in zip(bf(e), pb)]
        e = [x + y + z for x, y, z in zip(e, p, ep)]
        if it < INV_BASE_BITS - 2:
            p = p_next
    for mask in at_level:
        low = [jnp.where(mask, m, 0.0) for m in ms]
        eb = bf(e)
        x = [lo + _dot(lb, y) for lo, lb, y in zip(low, bf(low), eb)]
        ex = [_dot(y, xb) for y, xb in zip(eb, bf(x))]
        e = [ei - (xi + yi) for ei, xi, yi in zip(e, x, ex)]
    return e


def _gdn_out(o_heads, z, x, og_ref, wout_ref):
    normed = [oh * lax.rsqrt(jnp.mean(oh * oh, axis=-1, keepdims=True) + EPS) * og_ref[...]
              for oh in o_heads]
    gated = jnp.concatenate(normed, axis=1) * _silu_of_half(z)
    return _dot(gated.astype(BF16), wout_ref[...]) + x


def _gdn_delta_prompt_body(q_ref, k_ref, kb_ref, vb_ref, gam_ref, z_ref, x_ref, og_ref, wout_ref,
                           y_ref, sfin_ref, s_ref, o_ref, *, tm, n_t):
    t = pl.program_id(1)

    @pl.when(t == 0)
    def _():
        s_ref[...] = jnp.zeros_like(s_ref)

    c = DN_CHUNK
    ri = lax.broadcasted_iota(jnp.int32, (c, c), 0)
    ci = lax.broadcasted_iota(jnp.int32, (c, c), 1)
    incl = ri >= ci
    strict = ri > ci
    top = ri ^ ci
    in_base = (top >> INV_BASE_BITS) == 0
    levels = range(INV_BASE_BITS, c.bit_length() - 1)
    at_level = [(top >> lv) == 1 for lv in levels]
    n_chunks = tm // c
    pairs = [(n, hh) for n in range(n_chunks) for hh in range(DN_HEADS)]

    def blk(ref, n, hh):
        return ref[n * c:(n + 1) * c, hh * DN_DK:(hh + 1) * DN_DK]

    kh = [blk(k_ref, *p) for p in pairs]
    kbh = [blk(kb_ref, *p) for p in pairs]
    gam = [blk(gam_ref, *p) for p in pairs]
    qh = [blk(q_ref, *p) for p in pairs]
    kq = [_dot_nt(jnp.concatenate([a, q], axis=0), b) for a, q, b in zip(kbh, qh, kh)]
    kk = [x[:c] for x in kq]
    dec = [jnp.exp(jnp.where(incl, g - g.T, -1e30)) for g in gam]
    m = [jnp.where(strict, a * d, 0.0) for a, d in zip(kk, dec)]
    e = _unit_lower_inverse_minus_eye(m, in_base, at_level)
    eg = [jnp.exp(g) for g in gam]
    rhs = [jnp.concatenate([blk(vb_ref, *p).astype(F32), a.astype(F32) * x], axis=1)
           for p, a, x in zip(pairs, kbh, eg)]
    uw = [r + _dot(x.astype(BF16), r.astype(BF16)) for r, x in zip(rhs, e)]
    qkm = [(x[c:] * d).astype(BF16) for x, d in zip(kq, dec)]
    qd = [(a.astype(F32) * x).astype(BF16) for a, x in zip(qh, eg)]
    gl = [g[c - 1:c, :] for g in gam]
    kt = [(a.astype(F32) * jnp.exp(l - g)).astype(BF16) for a, l, g in zip(kh, gl, gam)]
    al = [jnp.exp(l) for l in gl]

    for n in range(n_chunks):
        idx = [n * DN_HEADS + hh for hh in range(DN_HEADS)]
        s_old = [s_ref[hh] for hh in range(DN_HEADS)]
        sb = [x.astype(BF16) for x in s_old]
        ws = [_dot(jnp.concatenate([uw[i][:, DN_DV:].astype(BF16), qd[i]], axis=0), y)
              for i, y in zip(idx, sb)]
        w = [uw[i][:, :DN_DV] - x[:c] for i, x in zip(idx, ws)]
        wb = [x.astype(BF16) for x in w]
        for hh, i in enumerate(idx):
            o_ref[n * c:(n + 1) * c, hh * DN_DV:(hh + 1) * DN_DV] = ws[hh][c:] + _dot(qkm[i], wb[hh])
        for hh, i in enumerate(idx):
            s_ref[hh] = al[i] * s_old[hh] + _dot_tn(kt[i], wb[hh])
    o = o_ref[...]
    o_heads = [o[:, hh * DN_DV:(hh + 1) * DN_DV] for hh in range(DN_HEADS)]
    y_ref[...] = _gdn_out(o_heads, z_ref[...], x_ref[...], og_ref, wout_ref)

    @pl.when(t == n_t - 1)
    def _():
        sfin_ref[...] = s_ref[...]


def _gdn_delta_prompt(q, k, kb, vb, gam, z, x, og, wout, *, tm):
    b, t, d = x.shape
    qk = DN_HEADS * DN_DK
    n_t = t // tm
    body = functools.partial(_gdn_delta_prompt_body, tm=tm, n_t=n_t)
    tok = lambda i, j: (i, j, 0)
    return pl.pallas_call(
        body,
        grid=(b, n_t),
        in_specs=[pl.BlockSpec((None, tm, qk), tok)] * 6 + [pl.BlockSpec((None, tm, d), tok),
                  _const_spec(og.shape), _const_spec(wout.shape)],
        out_specs=[pl.BlockSpec((None, tm, d), tok),
                   pl.BlockSpec((None, DN_HEADS, DN_DK, DN_DV), lambda i, j: (i, 0, 0, 0))],
        out_shape=[jax.ShapeDtypeStruct((b, t, d), F32),
                   jax.ShapeDtypeStruct((b, DN_HEADS, DN_DK, DN_DV), F32)],
        scratch_shapes=[pltpu.VMEM((DN_HEADS, DN_DK, DN_DV), F32),
                        pltpu.VMEM((tm, DN_HEADS * DN_DV), F32)],
        compiler_params=_params(2),
        name="gdn_delta_prompt",
    )(q, k, kb, vb, gam, z, x, og, wout)


def _gdn_proj_sample_body(x_ref, cst_ref, gain_ref, win_ref, cw_ref, alog_ref, dtb_ref,
                          q_ref, k_ref, v_ref, eg_ref, beta_ref, z_ref, ncst_ref):
    qk = DN_HEADS * DN_DK
    cch = 3 * qk
    nb = x_ref.shape[0]
    x = x_ref[...]
    h = _rmsnorm(x, gain_ref[...]).astype(BF16)
    proj = _dot(h, win_ref[...])
    qkv = proj[:, :cch]
    z_ref[...] = proj[:, cch:cch + qk]
    ab = proj[:, cch + qk:cch + qk + LANES]
    cw = 0.5 * cw_ref[...]
    conv = qkv * cw[DN_CONV - 1:DN_CONV, :]
    for j in range(DN_CONV - 1):
        conv = conv + cst_ref[j] * cw[j:j + 1, :]
    c = _silu_of_half(conv)
    ncst_ref[:DN_CONV - 2] = cst_ref[1:]
    ncst_ref[DN_CONV - 2] = qkv
    g, beta = _gdn_gates(ab, alog_ref, dtb_ref)

    def emit(hh, qn, kn, vh, g_col, b_col):
        q_ref[:, hh, :] = qn
        k_ref[:, hh, :] = kn
        v_ref[:, hh, :] = vh
        eg_ref[:, hh, :] = jnp.broadcast_to(jnp.exp(g_col), (nb, DN_DK))
        beta_ref[:, hh, :] = jnp.broadcast_to(b_col, (nb, DN_DK))

    _gdn_heads(c, g, beta, emit)


def _gdn_proj_sample(x, cst, gain, win, cw, alog, dtb):
    nb, d = x.shape
    qk = DN_HEADS * DN_DK
    return pl.pallas_call(
        _gdn_proj_sample_body,
        out_shape=[jax.ShapeDtypeStruct((nb, DN_HEADS, DN_DK), F32)] * 5 + [
            jax.ShapeDtypeStruct((nb, qk), F32), jax.ShapeDtypeStruct(cst.shape, F32)],
        compiler_params=_params(0),
        name="gdn_proj_sample",
    )(x, cst, gain, win, cw, alog, dtb)


def _gdn_state_sample_body(q_ref, k_ref, v_ref, eg_ref, beta_ref, s_ref, o_ref, ns_ref, *, nblk):
    n_pair = DN_HEADS // 2
    row = lax.broadcasted_iota(jnp.int32, (DN_HEADS, n_pair * DN_DK), 0)
    col_pair = lax.broadcasted_iota(jnp.int32, (DN_HEADS, n_pair * DN_DK), 1) // DN_DK
    in_pair = (row // 2) == col_pair
    even = (lax.broadcasted_iota(jnp.int32, (DN_HEADS, DN_DV), 0) % 2) == 0

    def by_pair(a):
        return jnp.where(in_pair, jnp.concatenate([a] * n_pair, axis=1), 0.0).astype(BF16)

    def paired(s):
        return jnp.concatenate(
            [jnp.concatenate([s[2 * p], s[2 * p + 1]], axis=1) for p in range(n_pair)], axis=0)

    def own_half(a):
        return jnp.where(even, a[:, :DN_DV], a[:, DN_DV:])

    seqs = range(nblk)
    s_old = [[s_ref[r, hh] for hh in range(DN_HEADS)] for r in seqs]
    kp = [by_pair(k_ref[r]) for r in seqs]
    eg = [eg_ref[r] for r in seqs]
    kts = [own_half(_dot(kp[r], paired(s_old[r]).astype(BF16))) for r in seqs]
    w = [beta_ref[r] * (v_ref[r] - eg[r] * kts[r]) for r in seqs]
    w2 = [jnp.concatenate([jnp.where(even, x, 0.0), jnp.where(even, 0.0, x)], axis=1) for x in w]
    outer = [_dot_tn(kp[r], w2[r].astype(BF16)) for r in seqs]
    s_new = []
    for r in seqs:
        s_r = []
        for hh in range(DN_HEADS):
            p, b = divmod(hh, 2)
            blk = outer[r][p * DN_DK:(p + 1) * DN_DK, b * DN_DV:(b + 1) * DN_DV]
            s_r.append(s_old[r][hh] * eg[r][hh:hh + 1, :] + blk)
            ns_ref[r, hh] = s_r[hh]
        s_new.append(s_r)
    for r in seqs:
        o_ref[r] = own_half(_dot(by_pair(q_ref[r]), paired(s_new[r]).astype(BF16)))


def _gdn_state_sample(q, k, v, eg, beta, s, *, nblk):
    nb = s.shape[0]
    body = functools.partial(_gdn_state_sample_body, nblk=nblk)
    vec = pl.BlockSpec((nblk, DN_HEADS, DN_DK), lambda i: (i, 0, 0))
    st = pl.BlockSpec((nblk, DN_HEADS, DN_DK, DN_DV), lambda i: (i, 0, 0, 0))
    return pl.pallas_call(
        body,
        grid=(nb // nblk,),
        in_specs=[vec] * 5 + [st],
        out_specs=[vec, st],
        out_shape=[jax.ShapeDtypeStruct((nb, DN_HEADS, DN_DV), F32),
                   jax.ShapeDtypeStruct(s.shape, F32)],
        compiler_params=_params(1),
        name="gdn_state_sample",
    )(q, k, v, eg, beta, s)


def _gdn_out_sample_body(o_ref, z_ref, x_ref, og_ref, wout_ref, y_ref):
    o_heads = [o_ref[:, hh, :] for hh in range(DN_HEADS)]
    y_ref[...] = _gdn_out(o_heads, z_ref[...], x_ref[...], og_ref, wout_ref)


def _gdn_out_sample(o, z, x, og, wout):
    return pl.pallas_call(
        _gdn_out_sample_body,
        out_shape=jax.ShapeDtypeStruct(x.shape, F32),
        compiler_params=_params(0),
        name="gdn_out_sample",
    )(o, z, x, og, wout)


def _row(a):
    return a.reshape(1, -1).astype(F32)


def _time_major(st):
    return jnp.transpose(st, (1, 0, 2))


def _forward(x_prompt, x_sample, state_pool_l0, state_conv_l2, state_delta_l2, state_pool_l3,
             l0_norm, l0_pool_w_in, l0_pool_w_grp, l0_pool_scale, l0_pool_w_out,
             l1_norm, l1_sgu_w_in, l1_sgu_ln_g, l1_sgu_ln_b, l1_sgu_w_s, l1_sgu_b_s, l1_sgu_w_out,
             l2_norm, l2_dn_w_in, l2_dn_conv_w, l2_dn_a_log, l2_dn_dt_bias, l2_dn_o_gain, l2_dn_w_out,
             l3_norm, l3_pool_w_in, l3_pool_w_grp, l3_pool_scale, l3_pool_w_out,
             final_norm, *, tm_pool, tm_sgu, tm_gdn_proj, tm_gdn, nblk_state):
    b, t, d = x_prompt.shape
    nb = x_sample.shape[0]
    dp = l0_pool_scale.shape[0]
    ds = l1_sgu_ln_g.shape[0]
    gw_s = ds // SGU_GROUPS
    qk = DN_HEADS * DN_DK
    cch = 3 * qk
    fin = _row(final_norm)
    xs = x_sample

    def pool_layer(xp, xs_, st, norm, w_in, w_grp, scale, w_out, final):
        gain, sc = _row(norm), _row(scale)
        yp, st_p, win_b, wgrp_b, wout_b = _pool_prompt(
            xp, gain, sc, fin, w_in, w_grp.reshape(-1, w_grp.shape[-1]), w_out,
            tm=tm_pool, final_norm=final)
        ys, st_s = _pool_sample(xs_, _time_major(st), gain, win_b, wgrp_b, sc, wout_b, fin,
                                final_norm=final, y_shape=(nb, 1, d) if final else (nb, d))
        return yp, ys, st_p[:, POOL_HIST_PAD - POOL_HIST:, :], _time_major(st_s)

    xp, xs, pool0_p, pool0_s = pool_layer(
        x_prompt, xs, state_pool_l0, l0_norm, l0_pool_w_in, l0_pool_w_grp, l0_pool_scale,
        l0_pool_w_out, False)

    bsb = jnp.repeat(l1_sgu_b_s.T.astype(F32), gw_s, axis=1)
    xp, sgu_w_in, sgu_w_out = _sgu_prompt(
        xp.reshape(b * t, d), _row(l1_norm), _row(l1_sgu_ln_g), _row(l1_sgu_ln_b),
        l1_sgu_w_s.astype(F32), bsb, l1_sgu_w_in, l1_sgu_w_out, tm=tm_sgu)
    xp = xp.reshape(b, t, d)
    w00 = jnp.repeat(l1_sgu_w_s[:, 0, 0].astype(F32), gw_s)[None, :]
    b0 = jnp.repeat(l1_sgu_b_s[:, 0].astype(F32), gw_s)[None, :]
    xs, sgu1_s = _sgu_sample(xs, _row(l1_norm), sgu_w_in, _row(l1_sgu_ln_g), _row(l1_sgu_ln_b),
                             w00, b0, sgu_w_out)

    n_proj = l2_dn_w_in.shape[1]
    n_pad = cch + qk + LANES
    z_cols = (jnp.arange(n_proj) >= cch) & (jnp.arange(n_proj) < cch + qk)
    dn_w_in = l2_dn_w_in * jnp.where(z_cols, 0.5, 1.0).astype(F32)[None, :]
    dn_w_in = jnp.pad(dn_w_in, ((0, 0), (0, n_pad - n_proj))).astype(BF16)
    dn_w_out = l2_dn_w_out.astype(BF16)
    alog = jnp.pad(l2_dn_a_log.astype(F32), (0, LANES - DN_HEADS))[None, :]
    dtb = jnp.pad(l2_dn_dt_bias.astype(F32), (0, LANES - DN_HEADS))[None, :]
    cw = l2_dn_conv_w.astype(F32)
    og = _row(l2_dn_o_gain)
    q, k, kb, vb, gam, z, conv2_p = _gdn_proj_prompt(xp, _row(l2_norm), dn_w_in, cw, alog, dtb,
                                                     tm=tm_gdn_proj)
    xp, dn2_p = _gdn_delta_prompt(q, k, kb, vb, gam, z, xp, og, dn_w_out, tm=tm_gdn)
    conv2_p = conv2_p[:, CONV_HIST_PAD - (DN_CONV - 1):, :]

    qs, ks, vs, egs, betas, zs, conv2_s = _gdn_proj_sample(
        xs, _time_major(state_conv_l2), _row(l2_norm), dn_w_in, cw, alog, dtb)
    conv2_s = _time_major(conv2_s)
    o_s, dn2_s = _gdn_state_sample(qs, ks, vs, egs, betas, state_delta_l2, nblk=nblk_state)
    xs = _gdn_out_sample(o_s, zs, xs, og, dn_w_out)

    yp, ys, pool3_p, pool3_s = pool_layer(
        xp, xs, state_pool_l3, l3_norm, l3_pool_w_in, l3_pool_w_grp, l3_pool_scale,
        l3_pool_w_out, True)

    return (yp, ys, pool0_p, pool0_s, sgu1_s,
            conv2_p, conv2_s, dn2_p, dn2_s, pool3_p, pool3_s)


def kernel(x_prompt, x_sample, state_pool_l0, state_conv_l2, state_delta_l2, state_pool_l3, l0_norm, l0_pool_w_in, l0_pool_w_grp, l0_pool_scale, l0_pool_w_out, l1_norm, l1_sgu_w_in, l1_sgu_ln_g, l1_sgu_ln_b, l1_sgu_w_s, l1_sgu_b_s, l1_sgu_w_out, l2_norm, l2_dn_w_in, l2_dn_conv_w, l2_dn_a_log, l2_dn_dt_bias, l2_dn_o_gain, l2_dn_w_out, l3_norm, l3_pool_w_in, l3_pool_w_grp, l3_pool_scale, l3_pool_w_out, final_norm):
    return _forward(
        x_prompt, x_sample, state_pool_l0, state_conv_l2, state_delta_l2, state_pool_l3,
        l0_norm, l0_pool_w_in, l0_pool_w_grp, l0_pool_scale, l0_pool_w_out,
        l1_norm, l1_sgu_w_in, l1_sgu_ln_g, l1_sgu_ln_b, l1_sgu_w_s, l1_sgu_b_s, l1_sgu_w_out,
        l2_norm, l2_dn_w_in, l2_dn_conv_w, l2_dn_a_log, l2_dn_dt_bias, l2_dn_o_gain, l2_dn_w_out,
        l3_norm, l3_pool_w_in, l3_pool_w_grp, l3_pool_scale, l3_pool_w_out,
        final_norm, tm_pool=1024, tm_sgu=512, tm_gdn_proj=512, tm_gdn=256, nblk_state=16)
```

```python
import functools

import jax
import jax.numpy as jnp
from jax import lax
from jax.experimental import pallas as pl
from jax.experimental.pallas import tpu as pltpu

F32 = jnp.float32
BF16 = jnp.bfloat16
EPS = 1e-6
PAST_LEN = 16384
POOL_WINDOWS = (2, 4, 8, 16)
POOL_HIST = 15
POOL_HIST_PAD = 16
SGU_CHUNK = 128
SGU_GROUPS = 4
DN_HEADS = 8
DN_DK = 128
DN_DV = 128
DN_CONV = 4
CONV_HIST_PAD = 8
DN_CHUNK = 128
INV_BASE_BITS = 4
LANES = 128
N_PREP = 8
VMEM_LIMIT = 60 * 1024 * 1024


def _rmsnorm(x, gain):
    ms = jnp.mean(x * x, axis=-1, keepdims=True)
    return x * lax.rsqrt(ms + EPS) * gain


def _silu_of_half(h):
    return h + h * jnp.tanh(h)


def _silu(z):
    return _silu_of_half(0.5 * z)


def _gelu(x):
    return 0.5 * x * (1.0 + lax.erf(x * (2.0 ** -0.5)))


def _dot(a, b):
    return jnp.dot(a, b, preferred_element_type=F32)


def _dot_nt(a, b):
    return lax.dot_general(a, b, (((1,), (1,)), ((), ())), preferred_element_type=F32)


def _dot_tn(a, b):
    return lax.dot_general(a, b, (((0,), (0,)), ((), ())), preferred_element_type=F32)


def _load_rows(ref):
    return ref[:, 0, :] if len(ref.shape) == 3 else ref[...]


def _store_rows(ref, val):
    if len(ref.shape) == 3:
        ref[:, 0, :] = val
    else:
        ref[...] = val


def _const_spec(shape):
    nd = len(shape)
    return pl.BlockSpec(shape, lambda *_: (0,) * nd, pipeline_mode=pl.Buffered(1))


def _params(n_grid):
    return pltpu.CompilerParams(
        dimension_semantics=("arbitrary",) * n_grid, vmem_limit_bytes=VMEM_LIMIT)


def _prep_specs(w):
    rows, cols = w.shape
    chunk = (rows // N_PREP, cols)
    index = lambda i: (jnp.minimum(i, N_PREP - 1), 0)
    return (pl.BlockSpec(chunk, index), pl.BlockSpec(chunk, index),
            jax.ShapeDtypeStruct(w.shape, BF16), pltpu.VMEM(w.shape, BF16))


def _prep_weights(step, f32_refs, out_refs, resident_refs):
    for f_ref, o_ref, r_ref in zip(f32_refs, out_refs, resident_refs):
        rows = f_ref.shape[0]
        chunk = f_ref[...].astype(BF16)
        o_ref[...] = chunk
        r_ref[pl.ds(pl.multiple_of(step * rows, rows), rows), :] = chunk


def _token_step(i, n_t):
    j = jnp.maximum(i - N_PREP, 0)
    return j // n_t, j % n_t


def _pool_tail(pooled_groups, z, x, wgrp_ref, scale_ref, wout_ref, fin_ref, final_norm):
    gw = pooled_groups[0].shape[1]
    mixed = [_dot(p.astype(BF16), wgrp_ref[gi * gw:(gi + 1) * gw, :]) for gi, p in enumerate(pooled_groups)]
    mixed = jnp.concatenate(mixed, axis=1)
    gated = mixed * scale_ref[...] * _silu(z)
    y = _dot(gated.astype(BF16), wout_ref[...]) + x
    if final_norm:
        y = _rmsnorm(y, fin_ref[...])
    return y


def _pool_prompt_body(x_ref, gain_ref, scale_ref, fin_ref, winf_ref, wgrpf_ref, woutf_ref,
                      y_ref, st_ref, wino_ref, wgrpo_ref, wouto_ref,
                      win_ref, wgrp_ref, wout_ref, hist_ref, *, tm, n_t, final_norm):
    i = pl.program_id(0)

    @pl.when(i < N_PREP)
    def _():
        _prep_weights(i, (winf_ref, wgrpf_ref, woutf_ref), (wino_ref, wgrpo_ref, wouto_ref),
                      (win_ref, wgrp_ref, wout_ref))

    @pl.when(i >= N_PREP)
    def _():
        _pool_prompt_tile(_token_step(i, n_t)[1], x_ref, gain_ref, scale_ref, fin_ref, y_ref, st_ref,
                          win_ref, wgrp_ref, wout_ref, hist_ref, tm=tm, n_t=n_t, final_norm=final_norm)


def _pool_prompt_tile(t, x_ref, gain_ref, scale_ref, fin_ref, y_ref, st_ref,
                      win_ref, wgrp_ref, wout_ref, hist_ref, *, tm, n_t, final_norm):
    dp = scale_ref.shape[-1]
    gw = dp // len(POOL_WINDOWS)

    @pl.when(t == 0)
    def _():
        hist_ref[...] = jnp.zeros_like(hist_ref)

    x = x_ref[...]
    h = _rmsnorm(x, gain_ref[...]).astype(BF16)
    xz = _dot(h, win_ref[...])
    xb = xz[:, :dp]
    z = xz[:, dp:]
    ext = jnp.concatenate([hist_ref[...], xb], axis=0)
    last = xb[tm - POOL_HIST_PAD:, :]
    hist_ref[...] = last
    pos1 = lax.broadcasted_iota(jnp.int32, (tm, LANES), 0) + (t * tm + 1)
    pooled = []
    for gi, w in enumerate(POOL_WINDOWS):
        e = ext[:, gi * gw:(gi + 1) * gw]
        s = e + pltpu.roll(e, 1, 0)
        sh = 2
        while sh < w:
            s = s + pltpu.roll(s, sh, 0)
            sh *= 2
        s = s[POOL_HIST_PAD:, :]
        inv = 1.0 / jnp.minimum(pos1, w).astype(F32)
        inv = jnp.concatenate([inv] * (gw // LANES), axis=1)
        pooled.append(s * inv - xb[:, gi * gw:(gi + 1) * gw])
    y_ref[...] = _pool_tail(pooled, z, x, wgrp_ref, scale_ref, wout_ref, fin_ref, final_norm)

    @pl.when(t == n_t - 1)
    def _():
        st_ref[...] = last


def _pool_prompt(x, gain, scale, fin, win, wgrp, wout, *, tm, final_norm):
    b, t, d = x.shape
    dp = scale.shape[-1]
    n_t = t // tm
    body = functools.partial(_pool_prompt_body, tm=tm, n_t=n_t, final_norm=final_norm)
    w_in_specs, w_out_specs, w_out_shapes, w_scratch = zip(*[_prep_specs(w) for w in (win, wgrp, wout)])
    tok = lambda i: _token_step(i, n_t) + (0,)
    return pl.pallas_call(
        body,
        grid=(N_PREP + b * n_t,),
        in_specs=[pl.BlockSpec((None, tm, d), tok), _const_spec(gain.shape), _const_spec(scale.shape),
                  _const_spec(fin.shape), *w_in_specs],
        out_specs=[pl.BlockSpec((None, tm, d), tok),
                   pl.BlockSpec((None, POOL_HIST_PAD, dp), lambda i: (_token_step(i, n_t)[0], 0, 0)),
                   *w_out_specs],
        out_shape=[jax.ShapeDtypeStruct((b, t, d), F32),
                   jax.ShapeDtypeStruct((b, POOL_HIST_PAD, dp), F32), *w_out_shapes],
        scratch_shapes=[*w_scratch, pltpu.VMEM((POOL_HIST_PAD, dp), F32)],
        compiler_params=_params(1),
        name="pool_prompt",
    )(x, gain, scale, fin, win, wgrp, wout)


def _pool_sample_body(x_ref, st_ref, gain_ref, win_ref, wgrp_ref, scale_ref, wout_ref, fin_ref,
                      y_ref, nst_ref, *, final_norm):
    dp = scale_ref.shape[-1]
    gw = dp // len(POOL_WINDOWS)
    x = _load_rows(x_ref)
    h = _rmsnorm(x, gain_ref[...]).astype(BF16)
    xz = _dot(h, win_ref[...])
    xb = xz[:, :dp]
    z = xz[:, dp:]
    pooled = []
    for gi, w in enumerate(POOL_WINDOWS):
        xg = xb[:, gi * gw:(gi + 1) * gw]
        s = xg
        for k in range(1, w):
            s = s + st_ref[POOL_HIST - k, :, gi * gw:(gi + 1) * gw]
        cnt = float(min(PAST_LEN + 1, w))
        pooled.append(s * (1.0 / cnt) - xg)
    _store_rows(y_ref, _pool_tail(pooled, z, x, wgrp_ref, scale_ref, wout_ref, fin_ref, final_norm))
    nst_ref[:POOL_HIST - 1] = st_ref[1:]
    nst_ref[POOL_HIST - 1] = xb


def _pool_sample(x, st, gain, win, wgrp, scale, wout, fin, *, final_norm, y_shape):
    body = functools.partial(_pool_sample_body, final_norm=final_norm)
    return pl.pallas_call(
        body,
        out_shape=[jax.ShapeDtypeStruct(y_shape, F32), jax.ShapeDtypeStruct(st.shape, F32)],
        compiler_params=_params(0),
        name="pool_sample",
    )(x, st, gain, win, wgrp, scale, wout, fin)


def _sgu_front(x, gain_ref, win_ref, lng_ref, lnb_ref):
    ds = lng_ref.shape[-1]
    h = _rmsnorm(x, gain_ref[...]).astype(BF16)
    uvz = _dot(h, win_ref[...])
    u = _gelu(uvz[:, :ds])
    v = _gelu(uvz[:, ds:2 * ds])
    z = _silu(uvz[:, 2 * ds:])
    mu = jnp.mean(v, axis=-1, keepdims=True)
    vc = v - mu
    var = jnp.mean(vc * vc, axis=-1, keepdims=True)
    vn = vc * lax.rsqrt(var + EPS) * lng_ref[...] + lnb_ref[...]
    return u, vn, z


def _sgu_prompt_body(x_ref, gain_ref, lng_ref, lnb_ref, ws_ref, bsb_ref, winf_ref, woutf_ref,
                     y_ref, wino_ref, wouto_ref, win_ref, wout_ref, *, tm):
    i = pl.program_id(0)

    @pl.when(i < N_PREP)
    def _():
        _prep_weights(i, (winf_ref, woutf_ref), (wino_ref, wouto_ref), (win_ref, wout_ref))

    @pl.when(i >= N_PREP)
    def _():
        _sgu_prompt_tile(x_ref, gain_ref, win_ref, lng_ref, lnb_ref, ws_ref, bsb_ref, wout_ref, y_ref, tm=tm)


def _sgu_prompt_tile(x_ref, gain_ref, win_ref, lng_ref, lnb_ref, ws_ref, bsb_ref, wout_ref,
                     y_ref, *, tm):
    ds = lng_ref.shape[-1]
    gw = ds // SGU_GROUPS
    x = x_ref[...]
    u, vn, z = _sgu_front(x, gain_ref, win_ref, lng_ref, lnb_ref)
    ri = lax.broadcasted_iota(jnp.int32, (SGU_CHUNK, SGU_CHUNK), 0)
    ci = lax.broadcasted_iota(jnp.int32, (SGU_CHUNK, SGU_CHUNK), 1)
    ws = [jnp.where(ri >= ci, ws_ref[g], 0.0).astype(BF16) for g in range(SGU_GROUPS)]
    vnb = vn.astype(BF16)
    bsb = bsb_ref[...]
    rows = []
    for n in range(tm // SGU_CHUNK):
        r0 = n * SGU_CHUNK
        cols = [_dot(ws[g], vnb[r0:r0 + SGU_CHUNK, g * gw:(g + 1) * gw]) for g in range(SGU_GROUPS)]
        rows.append(jnp.concatenate(cols, axis=1) + bsb)
    s = jnp.concatenate(rows, axis=0)
    gated = u * s * z
    y_ref[...] = _dot(gated.astype(BF16), wout_ref[...]) + x


def _sgu_prompt(x, gain, lng, lnb, ws, bsb, win, wout, *, tm):
    n, d = x.shape
    body = functools.partial(_sgu_prompt_body, tm=tm)
    w_in_specs, w_out_specs, w_out_shapes, w_scratch = zip(*[_prep_specs(w) for w in (win, wout)])
    tok = lambda i: (jnp.maximum(i - N_PREP, 0), 0)
    return pl.pallas_call(
        body,
        grid=(N_PREP + n // tm,),
        in_specs=[pl.BlockSpec((tm, d), tok)] + [
            _const_spec(a.shape) for a in (gain, lng, lnb, ws, bsb)] + [*w_in_specs],
        out_specs=[pl.BlockSpec((tm, d), tok), *w_out_specs],
        out_shape=[jax.ShapeDtypeStruct((n, d), F32), *w_out_shapes],
        scratch_shapes=[*w_scratch],
        compiler_params=_params(1),
        name="sgu_prompt",
    )(x, gain, lng, lnb, ws, bsb, win, wout)


def _sgu_sample_body(x_ref, gain_ref, win_ref, lng_ref, lnb_ref, w00_ref, b0_ref, wout_ref,
                     y_ref, v_ref):
    x = x_ref[...]
    u, vn, z = _sgu_front(x, gain_ref, win_ref, lng_ref, lnb_ref)
    s = vn * w00_ref[...] + b0_ref[...]
    gated = u * s * z
    y_ref[...] = _dot(gated.astype(BF16), wout_ref[...]) + x
    _store_rows(v_ref, vn)


def _sgu_sample(x, gain, win, lng, lnb, w00, b0, wout):
    nb, d = x.shape
    ds = lng.shape[-1]
    return pl.pallas_call(
        _sgu_sample_body,
        out_shape=[jax.ShapeDtypeStruct((nb, d), F32), jax.ShapeDtypeStruct((nb, 1, ds), F32)],
        compiler_params=_params(0),
        name="sgu_sample",
    )(x, gain, win, lng, lnb, w00, b0, wout)


def _gdn_gates(ab, alog_ref, dtb_ref):
    g = -jnp.exp(alog_ref[...]) * jax.nn.softplus(ab + dtb_ref[...])
    beta = jax.nn.sigmoid(ab)
    return g, beta


def _lane_pick(a, lane, idx):
    return jnp.sum(jnp.where(lane == idx, a, 0.0), axis=-1, keepdims=True)


def _gdn_heads(c, g_like, beta_all, emit):
    rows = c.shape[0]
    qk = DN_HEADS * DN_DK
    lane = lax.broadcasted_iota(jnp.int32, (rows, LANES), 1)
    for hh in range(DN_HEADS):
        qh = c[:, hh * DN_DK:(hh + 1) * DN_DK]
        kh = c[:, qk + hh * DN_DK:qk + (hh + 1) * DN_DK]
        vh = c[:, 2 * qk + hh * DN_DV:2 * qk + (hh + 1) * DN_DV]
        qn = qh * (lax.rsqrt(jnp.sum(qh * qh, axis=-1, keepdims=True) + EPS) * (DN_DK ** -0.5))
        kn = kh * lax.rsqrt(jnp.sum(kh * kh, axis=-1, keepdims=True) + EPS)
        emit(hh, qn, kn, vh, _lane_pick(g_like, lane, hh), _lane_pick(beta_all, lane, DN_HEADS + hh))


def _gdn_proj_prompt_body(x_ref, gain_ref, cw_ref, alog_ref, dtb_ref, wtf_ref, wtail_ref,
                          q_ref, k_ref, kb_ref, vb_ref, gam_ref, z_ref, cst_ref, wo_ref, wotail_ref,
                          win_ref, hist_ref, *, tm, n_t):
    i = pl.program_id(0)
    qk = DN_HEADS * DN_DK
    cch = 3 * qk
    rows = wtf_ref.shape[0]

    @pl.when(i < N_PREP)
    def _():
        scale = jnp.where(i * rows >= cch, 0.5, 1.0).astype(F32)
        chunk = (wtf_ref[...] * scale).T.astype(BF16)
        wo_ref[...] = chunk
        win_ref[:, pl.ds(pl.multiple_of(i * rows, rows), rows)] = chunk

    @pl.when(i == 0)
    def _():
        tail = wtail_ref[...]
        tail = jnp.concatenate([tail, jnp.zeros((LANES - tail.shape[0], tail.shape[1]), F32)], axis=0)
        tail = tail.T.astype(BF16)
        wotail_ref[...] = tail
        win_ref[:, cch + qk:] = tail

    @pl.when(i >= N_PREP)
    def _():
        _gdn_proj_prompt_tile(_token_step(i, n_t)[1], x_ref, gain_ref, win_ref, cw_ref, alog_ref, dtb_ref,
                              q_ref, k_ref, kb_ref, vb_ref, gam_ref, z_ref, cst_ref, hist_ref, tm=tm, n_t=n_t)


def _gdn_proj_prompt_tile(t, x_ref, gain_ref, win_ref, cw_ref, alog_ref, dtb_ref,
                          q_ref, k_ref, kb_ref, vb_ref, gam_ref, z_ref, cst_ref, hist_ref,
                          *, tm, n_t):
    qk = DN_HEADS * DN_DK
    cch = 3 * qk

    @pl.when(t == 0)
    def _():
        hist_ref[...] = jnp.zeros_like(hist_ref)

    x = x_ref[...]
    h = _rmsnorm(x, gain_ref[...]).astype(BF16)
    proj = _dot(h, win_ref[...])
    qkv = proj[:, :cch]
    z_ref[...] = proj[:, cch:cch + qk]
    ab = proj[:, cch + qk:cch + qk + LANES]
    ext = jnp.concatenate([hist_ref[...], qkv], axis=0)
    last = qkv[tm - CONV_HIST_PAD:, :]
    hist_ref[...] = last
    cw = 0.5 * cw_ref[...]
    conv = ext * cw[DN_CONV - 1:DN_CONV, :]
    for j in range(1, DN_CONV):
        conv = conv + pltpu.roll(ext, j, 0) * cw[DN_CONV - 1 - j:DN_CONV - j, :]
    c = _silu_of_half(conv[CONV_HIST_PAD:, :])

    g, beta = _gdn_gates(ab, alog_ref, dtb_ref)
    rin = lax.broadcasted_iota(jnp.int32, (tm, LANES), 0) & (DN_CHUNK - 1)
    gam = g
    sh = 1
    while sh < DN_CHUNK:
        gam = gam + jnp.where(rin >= sh, pltpu.roll(gam, sh, 0), 0.0)
        sh *= 2

    def emit(hh, qn, kn, vh, g_col, b_col):
        sl = slice(hh * DN_DK, (hh + 1) * DN_DK)
        q_ref[:, sl] = qn.astype(BF16)
        k_ref[:, sl] = kn.astype(BF16)
        kb_ref[:, sl] = (b_col * kn).astype(BF16)
        vb_ref[:, sl] = (b_col * vh).astype(BF16)
        gam_ref[:, sl] = jnp.broadcast_to(g_col, (tm, DN_DK))

    _gdn_heads(c, gam, beta, emit)

    @pl.when(t == n_t - 1)
    def _():
        cst_ref[...] = last


def _gdn_proj_prompt(x, gain, wt, cw, alog, dtb, *, tm):
    b, t, d = x.shape
    qk = DN_HEADS * DN_DK
    cch = 3 * qk
    n_t = t // tm
    n_main = cch + qk
    rows = n_main // N_PREP
    n_tail = wt.shape[0] - n_main
    body = functools.partial(_gdn_proj_prompt_body, tm=tm, n_t=n_t)
    tok = lambda i: _token_step(i, n_t) + (0,)
    chunk = lambda i: jnp.minimum(i, N_PREP - 1)
    return pl.pallas_call(
        body,
        grid=(N_PREP + b * n_t,),
        in_specs=[pl.BlockSpec((None, tm, d), tok)] + [_const_spec(a.shape) for a in (gain, cw, alog, dtb)] + [
            pl.BlockSpec((rows, d), lambda i: (chunk(i), 0)),
            pl.BlockSpec((n_tail, d), lambda i: (n_main // n_tail, 0))],
        out_specs=[pl.BlockSpec((None, tm, qk), tok)] * 6 + [
            pl.BlockSpec((None, CONV_HIST_PAD, cch), lambda i: (_token_step(i, n_t)[0], 0, 0)),
            pl.BlockSpec((d, rows), lambda i: (0, chunk(i))),
            pl.BlockSpec((d, LANES), lambda i: (0, 0))],
        out_shape=[jax.ShapeDtypeStruct((b, t, qk), BF16)] * 4 + [
            jax.ShapeDtypeStruct((b, t, qk), F32)] * 2 + [
            jax.ShapeDtypeStruct((b, CONV_HIST_PAD, cch), F32),
            jax.ShapeDtypeStruct((d, n_main), BF16), jax.ShapeDtypeStruct((d, LANES), BF16)],
        scratch_shapes=[pltpu.VMEM((d, n_main + LANES), BF16), pltpu.VMEM((CONV_HIST_PAD, cch), F32)],
        compiler_params=_params(1),
        name="gdn_proj_prompt",
    )(x, gain, cw, alog, dtb, wt, wt)


def _unit_lower_inverse_minus_eye(ms, in_base, at_level):
    bf = lambda xs: [x.astype(BF16) for x in xs]
    a = [jnp.where(in_base, m, 0.0) for m in ms]
    e = [-x for x in a]
    ab = bf(a)
    p = [_dot(x, x) for x in ab]
    n = ms[0].shape[0]
    for it in range(INV_BASE_BITS - 1):
        pb = bf(p)
        if it < INV_BASE_BITS - 2:
            both = [_dot(jnp.concatenate([x, y], axis=0), y) for x, y in zip(bf(e), pb)]
            ep = [x[:n] for x in both]
            p_next = [x[n:] for x in both]
        else:
            ep = [_dot(x, y) for x, y in zip(bf(e), pb)]
        e = [x + y + z for x, y, z in zip(e, p, ep)]
        if it < INV_BASE_BITS - 2:
            p = p_next
    for mask in at_level:
        low = [jnp.where(mask, m, 0.0) for m in ms]
        eb = bf(e)
        x = [lo + _dot(lb, y) for lo, lb, y in zip(low, bf(low), eb)]
        ex = [_dot(y, xb) for y, xb in zip(eb, bf(x))]
        e = [ei - (xi + yi) for ei, xi, yi in zip(e, x, ex)]
    return e


def _gdn_out(o_heads, z, x, og_ref, wout_ref):
    normed = [oh * lax.rsqrt(jnp.mean(oh * oh, axis=-1, keepdims=True) + EPS) * og_ref[...]
              for oh in o_heads]
    gated = jnp.concatenate(normed, axis=1) * _silu_of_half(z)
    return _dot(gated.astype(BF16), wout_ref[...]) + x


def _gdn_delta_prompt_body(q_ref, k_ref, kb_ref, vb_ref, gam_ref, z_ref, x_ref, og_ref, wout_ref,
                           y_ref, sfin_ref, s_ref, o_ref, *, tm, n_t):
    t = pl.program_id(1)

    @pl.when(t == 0)
    def _():
        s_ref[...] = jnp.zeros_like(s_ref)

    c = DN_CHUNK
    ri = lax.broadcasted_iota(jnp.int32, (c, c), 0)
    ci = lax.broadcasted_iota(jnp.int32, (c, c), 1)
    incl = ri >= ci
    strict = ri > ci
    top = ri ^ ci
    in_base = (top >> INV_BASE_BITS) == 0
    levels = range(INV_BASE_BITS, c.bit_length() - 1)
    at_level = [(top >> lv) == 1 for lv in levels]
    n_chunks = tm // c
    pairs = [(n, hh) for n in range(n_chunks) for hh in range(DN_HEADS)]

    def blk(ref, n, hh):
        return ref[n * c:(n + 1) * c, hh * DN_DK:(hh + 1) * DN_DK]

    kh = [blk(k_ref, *p) for p in pairs]
    kbh = [blk(kb_ref, *p) for p in pairs]
    gam = [blk(gam_ref, *p) for p in pairs]
    qh = [blk(q_ref, *p) for p in pairs]
    kq = [_dot_nt(jnp.concatenate([a, q], axis=0), b) for a, q, b in zip(kbh, qh, kh)]
    kk = [x[:c] for x in kq]
    dec = [jnp.exp(jnp.where(incl, g - g.T, -1e30)) for g in gam]
    m = [jnp.where(strict, a * d, 0.0) for a, d in zip(kk, dec)]
    e = _unit_lower_inverse_minus_eye(m, in_base, at_level)
    eg = [jnp.exp(g) for g in gam]
    rhs = [jnp.concatenate([blk(vb_ref, *p).astype(F32), a.astype(F32) * x], axis=1)
           for p, a, x in zip(pairs, kbh, eg)]
    uw = [r + _dot(x.astype(BF16), r.astype(BF16)) for r, x in zip(rhs, e)]
    qkm = [(x[c:] * d).astype(BF16) for x, d in zip(kq, dec)]
    qd = [(a.astype(F32) * x).astype(BF16) for a, x in zip(qh, eg)]
    gl = [g[c - 1:c, :] for g in gam]
    kt = [(a.astype(F32) * jnp.exp(l - g)).astype(BF16) for a, l, g in zip(kh, gl, gam)]
    al = [jnp.exp(l) for l in gl]

    for n in range(n_chunks):
        idx = [n * DN_HEADS + hh for hh in range(DN_HEADS)]
        s_old = [s_ref[hh] for hh in range(DN_HEADS)]
        sb = [x.astype(BF16) for x in s_old]
        ws = [_dot(jnp.concatenate([uw[i][:, DN_DV:].astype(BF16), qd[i]], axis=0), y)
              for i, y in zip(idx, sb)]
        w = [uw[i][:, :DN_DV] - x[:c] for i, x in zip(idx, ws)]
        wb = [x.astype(BF16) for x in w]
        for hh, i in enumerate(idx):
            o_ref[n * c:(n + 1) * c, hh * DN_DV:(hh + 1) * DN_DV] = ws[hh][c:] + _dot(qkm[i], wb[hh])
        for hh, i in enumerate(idx):
            s_ref[hh] = al[i] * s_old[hh] + _dot_tn(kt[i], wb[hh])
    o = o_ref[...]
    o_heads = [o[:, hh * DN_DV:(hh + 1) * DN_DV] for hh in range(DN_HEADS)]
    y_ref[...] = _gdn_out(o_heads, z_ref[...], x_ref[...], og_ref, wout_ref)

    @pl.when(t == n_t - 1)
    def _():
        sfin_ref[...] = s_ref[...]


def _gdn_delta_prompt(q, k, kb, vb, gam, z, x, og, wout, *, tm):
    b, t, d = x.shape
    qk = DN_HEADS * DN_DK
    n_t = t // tm
    body = functools.partial(_gdn_delta_prompt_body, tm=tm, n_t=n_t)
    tok = lambda i, j: (i, j, 0)
    return pl.pallas_call(
        body,
        grid=(b, n_t),
        in_specs=[pl.BlockSpec((None, tm, qk), tok)] * 6 + [pl.BlockSpec((None, tm, d), tok),
                  _const_spec(og.shape), _const_spec(wout.shape)],
        out_specs=[pl.BlockSpec((None, tm, d), tok),
                   pl.BlockSpec((None, DN_HEADS, DN_DK, DN_DV), lambda i, j: (i, 0, 0, 0))],
        out_shape=[jax.ShapeDtypeStruct((b, t, d), F32),
                   jax.ShapeDtypeStruct((b, DN_HEADS, DN_DK, DN_DV), F32)],
        scratch_shapes=[pltpu.VMEM((DN_HEADS, DN_DK, DN_DV), F32),
                        pltpu.VMEM((tm, DN_HEADS * DN_DV), F32)],
        compiler_params=_params(2),
        name="gdn_delta_prompt",
    )(q, k, kb, vb, gam, z, x, og, wout)


def _gdn_proj_sample_body(x_ref, cst_ref, gain_ref, win_ref, wtail_ref, cw_ref, alog_ref, dtb_ref,
                          q_ref, k_ref, v_ref, eg_ref, beta_ref, z_ref, ncst_ref):
    qk = DN_HEADS * DN_DK
    cch = 3 * qk
    nb = x_ref.shape[0]
    x = x_ref[...]
    h = _rmsnorm(x, gain_ref[...]).astype(BF16)
    proj = _dot(h, win_ref[...])
    qkv = proj[:, :cch]
    z_ref[...] = proj[:, cch:cch + qk]
    ab = _dot(h, wtail_ref[...])
    cw = 0.5 * cw_ref[...]
    conv = qkv * cw[DN_CONV - 1:DN_CONV, :]
    for j in range(DN_CONV - 1):
        conv = conv + cst_ref[j] * cw[j:j + 1, :]
    c = _silu_of_half(conv)
    ncst_ref[:DN_CONV - 2] = cst_ref[1:]
    ncst_ref[DN_CONV - 2] = qkv
    g, beta = _gdn_gates(ab, alog_ref, dtb_ref)

    def emit(hh, qn, kn, vh, g_col, b_col):
        q_ref[:, hh, :] = qn
        k_ref[:, hh, :] = kn
        v_ref[:, hh, :] = vh
        eg_ref[:, hh, :] = jnp.broadcast_to(jnp.exp(g_col), (nb, DN_DK))
        beta_ref[:, hh, :] = jnp.broadcast_to(b_col, (nb, DN_DK))

    _gdn_heads(c, g, beta, emit)


def _gdn_proj_sample(x, cst, gain, win, wtail, cw, alog, dtb):
    nb, d = x.shape
    qk = DN_HEADS * DN_DK
    return pl.pallas_call(
        _gdn_proj_sample_body,
        out_shape=[jax.ShapeDtypeStruct((nb, DN_HEADS, DN_DK), F32)] * 5 + [
            jax.ShapeDtypeStruct((nb, qk), F32), jax.ShapeDtypeStruct(cst.shape, F32)],
        compiler_params=_params(0),
        name="gdn_proj_sample",
    )(x, cst, gain, win, wtail, cw, alog, dtb)


def _gdn_state_sample_body(q_ref, k_ref, v_ref, eg_ref, beta_ref, s_ref, o_ref, ns_ref, *, nblk):
    n_pair = DN_HEADS // 2
    row = lax.broadcasted_iota(jnp.int32, (DN_HEADS, n_pair * DN_DK), 0)
    col_pair = lax.broadcasted_iota(jnp.int32, (DN_HEADS, n_pair * DN_DK), 1) // DN_DK
    in_pair = (row // 2) == col_pair
    even = (lax.broadcasted_iota(jnp.int32, (DN_HEADS, DN_DV), 0) % 2) == 0

    def by_pair(a):
        return jnp.where(in_pair, jnp.concatenate([a] * n_pair, axis=1), 0.0).astype(BF16)

    def paired(s):
        return jnp.concatenate(
            [jnp.concatenate([s[2 * p], s[2 * p + 1]], axis=1) for p in range(n_pair)], axis=0)

    def own_half(a):
        return jnp.where(even, a[:, :DN_DV], a[:, DN_DV:])

    seqs = range(nblk)
    s_old = [[s_ref[r, hh] for hh in range(DN_HEADS)] for r in seqs]
    kp = [by_pair(k_ref[r]) for r in seqs]
    eg = [eg_ref[r] for r in seqs]
    kts = [own_half(_dot(kp[r], paired(s_old[r]).astype(BF16))) for r in seqs]
    w = [beta_ref[r] * (v_ref[r] - eg[r] * kts[r]) for r in seqs]
    w2 = [jnp.concatenate([jnp.where(even, x, 0.0), jnp.where(even, 0.0, x)], axis=1) for x in w]
    outer = [_dot_tn(kp[r], w2[r].astype(BF16)) for r in seqs]
    s_new = []
    for r in seqs:
        s_r = []
        for hh in range(DN_HEADS):
            p, b = divmod(hh, 2)
            blk = outer[r][p * DN_DK:(p + 1) * DN_DK, b * DN_DV:(b + 1) * DN_DV]
            s_r.append(s_old[r][hh] * eg[r][hh:hh + 1, :] + blk)
            ns_ref[r, hh] = s_r[hh]
        s_new.append(s_r)
    for r in seqs:
        o_ref[r] = own_half(_dot(by_pair(q_ref[r]), paired(s_new[r]).astype(BF16)))


def _gdn_state_sample(q, k, v, eg, beta, s, *, nblk):
    nb = s.shape[0]
    body = functools.partial(_gdn_state_sample_body, nblk=nblk)
    vec = pl.BlockSpec((nblk, DN_HEADS, DN_DK), lambda i: (i, 0, 0))
    st = pl.BlockSpec((nblk, DN_HEADS, DN_DK, DN_DV), lambda i: (i, 0, 0, 0))
    return pl.pallas_call(
        body,
        grid=(nb // nblk,),
        in_specs=[vec] * 5 + [st],
        out_specs=[vec, st],
        out_shape=[jax.ShapeDtypeStruct((nb, DN_HEADS, DN_DV), F32),
                   jax.ShapeDtypeStruct(s.shape, F32)],
        compiler_params=_params(1),
        name="gdn_state_sample",
    )(q, k, v, eg, beta, s)


def _gdn_out_sample_body(o_ref, z_ref, x_ref, og_ref, wout_ref, y_ref):
    o_heads = [o_ref[:, hh, :] for hh in range(DN_HEADS)]
    y_ref[...] = _gdn_out(o_heads, z_ref[...], x_ref[...], og_ref, wout_ref)


def _gdn_out_sample(o, z, x, og, wout):
    return pl.pallas_call(
        _gdn_out_sample_body,
        out_shape=jax.ShapeDtypeStruct(x.shape, F32),
        compiler_params=_params(0),
        name="gdn_out_sample",
    )(o, z, x, og, wout)


def _row(a):
    return a.reshape(1, -1).astype(F32)


def _time_major(st):
    return jnp.transpose(st, (1, 0, 2))


def _forward(x_prompt, x_sample, state_pool_l0, state_conv_l2, state_delta_l2, state_pool_l3,
             l0_norm, l0_pool_w_in, l0_pool_w_grp, l0_pool_scale, l0_pool_w_out,
             l1_norm, l1_sgu_w_in, l1_sgu_ln_g, l1_sgu_ln_b, l1_sgu_w_s, l1_sgu_b_s, l1_sgu_w_out,
             l2_norm, l2_dn_w_in, l2_dn_conv_w, l2_dn_a_log, l2_dn_dt_bias, l2_dn_o_gain, l2_dn_w_out,
             l3_norm, l3_pool_w_in, l3_pool_w_grp, l3_pool_scale, l3_pool_w_out,
             final_norm, *, tm_pool, tm_sgu, tm_gdn_proj, tm_gdn, nblk_state):
    b, t, d = x_prompt.shape
    nb = x_sample.shape[0]
    dp = l0_pool_scale.shape[0]
    ds = l1_sgu_ln_g.shape[0]
    gw_s = ds // SGU_GROUPS
    qk = DN_HEADS * DN_DK
    cch = 3 * qk
    fin = _row(final_norm)
    xs = x_sample

    def pool_layer(xp, xs_, st, norm, w_in, w_grp, scale, w_out, final):
        gain, sc = _row(norm), _row(scale)
        yp, st_p, win_b, wgrp_b, wout_b = _pool_prompt(
            xp, gain, sc, fin, w_in, w_grp.reshape(-1, w_grp.shape[-1]), w_out,
            tm=tm_pool, final_norm=final)
        ys, st_s = _pool_sample(xs_, _time_major(st), gain, win_b, wgrp_b, sc, wout_b, fin,
                                final_norm=final, y_shape=(nb, 1, d) if final else (nb, d))
        return yp, ys, st_p[:, POOL_HIST_PAD - POOL_HIST:, :], _time_major(st_s)

    xp, xs, pool0_p, pool0_s = pool_layer(
        x_prompt, xs, state_pool_l0, l0_norm, l0_pool_w_in, l0_pool_w_grp, l0_pool_scale,
        l0_pool_w_out, False)

    bsb = jnp.repeat(l1_sgu_b_s.T.astype(F32), gw_s, axis=1)
    xp, sgu_w_in, sgu_w_out = _sgu_prompt(
        xp.reshape(b * t, d), _row(l1_norm), _row(l1_sgu_ln_g), _row(l1_sgu_ln_b),
        l1_sgu_w_s.astype(F32), bsb, l1_sgu_w_in, l1_sgu_w_out, tm=tm_sgu)
    xp = xp.reshape(b, t, d)
    w00 = jnp.repeat(l1_sgu_w_s[:, 0, 0].astype(F32), gw_s)[None, :]
    b0 = jnp.repeat(l1_sgu_b_s[:, 0].astype(F32), gw_s)[None, :]
    xs, sgu1_s = _sgu_sample(xs, _row(l1_norm), sgu_w_in, _row(l1_sgu_ln_g), _row(l1_sgu_ln_b),
                             w00, b0, sgu_w_out)

    dn_w_out = l2_dn_w_out.astype(BF16)
    alog = jnp.pad(l2_dn_a_log.astype(F32), (0, LANES - DN_HEADS))[None, :]
    dtb = jnp.pad(l2_dn_dt_bias.astype(F32), (0, LANES - DN_HEADS))[None, :]
    cw = l2_dn_conv_w.astype(F32)
    og = _row(l2_dn_o_gain)
    q, k, kb, vb, gam, z, conv2_p, dn_w_in, dn_w_tail = _gdn_proj_prompt(
        xp, _row(l2_norm), l2_dn_w_in.T, cw, alog, dtb, tm=tm_gdn_proj)
    xp, dn2_p = _gdn_delta_prompt(q, k, kb, vb, gam, z, xp, og, dn_w_out, tm=tm_gdn)
    conv2_p = conv2_p[:, CONV_HIST_PAD - (DN_CONV - 1):, :]

    qs, ks, vs, egs, betas, zs, conv2_s = _gdn_proj_sample(
        xs, _time_major(state_conv_l2), _row(l2_norm), dn_w_in, dn_w_tail, cw, alog, dtb)
    conv2_s = _time_major(conv2_s)
    o_s, dn2_s = _gdn_state_sample(qs, ks, vs, egs, betas, state_delta_l2, nblk=nblk_state)
    xs = _gdn_out_sample(o_s, zs, xs, og, dn_w_out)

    yp, ys, pool3_p, pool3_s = pool_layer(
        xp, xs, state_pool_l3, l3_norm, l3_pool_w_in, l3_pool_w_grp, l3_pool_scale,
        l3_pool_w_out, True)

    return (yp, ys, pool0_p, pool0_s, sgu1_s,
            conv2_p, conv2_s, dn2_p, dn2_s, pool3_p, pool3_s)


def kernel(x_prompt, x_sample, state_pool_l0, state_conv_l2, state_delta_l2, state_pool_l3, l0_norm, l0_pool_w_in, l0_pool_w_grp, l0_pool_scale, l0_pool_w_out, l1_norm, l1_sgu_w_in, l1_sgu_ln_g, l1_sgu_ln_b, l1_sgu_w_s, l1_sgu_b_s, l1_sgu_w_out, l2_norm, l2_dn_w_in, l2_dn_conv_w, l2_dn_a_log, l2_dn_dt_bias, l2_dn_o_gain, l2_dn_w_out, l3_norm, l3_pool_w_in, l3_pool_w_grp, l3_pool_scale, l3_pool_w_out, final_norm):
    return _forward(
        x_prompt, x_sample, state_pool_l0, state_conv_l2, state_delta_l2, state_pool_l3,
        l0_norm, l0_pool_w_in, l0_pool_w_grp, l0_pool_scale, l0_pool_w_out,
        l1_norm, l1_sgu_w_in, l1_sgu_ln_g, l1_sgu_ln_b, l1_sgu_w_s, l1_sgu_b_s, l1_sgu_w_out,
        l2_norm, l2_dn_w_in, l2_dn_conv_w, l2_dn_a_log, l2_dn_dt_bias, l2_dn_o_gain, l2_dn_w_out,
        l3_norm, l3_pool_w_in, l3_pool_w_grp, l3_pool_scale, l3_pool_w_out,
        final_norm, tm_pool=1024, tm_sgu=512, tm_gdn_proj=512, tm_gdn=256, nblk_state=16)
```
